```python
import math
import jax, jax.numpy as jnp
from jax import lax
import numpy as np

D_MODEL = 4096
BATCH = 1
SEQ = 8192
DEPTH = 4

D_MIX = D_MODEL
DA_WIDTH = D_MIX // 2
DA_HEAD = 128
DA_HEADS = DA_WIDTH // (2 * DA_HEAD)
ROPE_DIM = DA_HEAD // 4
ROPE_THETA = 500000.0
RET_WIDTH = D_MIX // 4
RET_HEADS = 4
RET_V_DIM = RET_WIDTH // RET_HEADS
RET_QK_DIM = RET_V_DIM // 2
RET_QK_WIDTH = RET_HEADS * RET_QK_DIM
RET_THETA = 10000.0
CONV_CH = D_MIX - DA_WIDTH - RET_WIDTH
CONV_K = 31
Q_BLOCK = 128
CHUNK = 128
IN_COLS = 4 * DA_WIDTH + 2 * RET_QK_WIDTH + 2 * RET_WIDTH + 3 * CONV_CH
EPS = 1e-6

kernel_name = "hymba_diffattn_retnet_conformer_trunk"


def rms_norm(x, g=None, eps=EPS):
    xf = x.astype(jnp.float32)
    y = xf * lax.rsqrt(jnp.mean(xf * xf, axis=-1, keepdims=True) + eps)
    if g is not None:
        y = y * g.astype(jnp.float32)
    return y.astype(x.dtype)


def layer_norm(x, g, b, eps=1e-5):
    xf = x.astype(jnp.float32)
    mu = jnp.mean(xf, axis=-1, keepdims=True)
    var = jnp.mean(jnp.square(xf - mu), axis=-1, keepdims=True)
    y = (xf - mu) * lax.rsqrt(var + eps) * g.astype(jnp.float32) + b.astype(jnp.float32)
    return y.astype(x.dtype)


def rope(x, rot_dim, theta):
    S = x.shape[-2]
    pos = jnp.arange(S, dtype=jnp.float32)
    inv = 1.0 / (theta ** (jnp.arange(0, rot_dim, 2, dtype=jnp.float32) / rot_dim))
    ang = pos[:, None] * inv[None, :]
    cos, sin = jnp.cos(ang), jnp.sin(ang)
    xf = x.astype(jnp.float32)
    half = rot_dim // 2
    x1, x2, xp = xf[..., :half], xf[..., half:rot_dim], xf[..., rot_dim:]
    out = jnp.concatenate([x1 * cos - x2 * sin, x2 * cos + x1 * sin, xp], axis=-1)
    return out.astype(x.dtype)


def diff_attention(q, k, v, lam):
    B, H, _, S, d = q.shape
    nb = S // Q_BLOCK
    scale = d ** -0.5
    qb = jnp.moveaxis(q.reshape(B, H, 2, nb, Q_BLOCK, d), 3, 0)
    kpos = jnp.arange(S)

    def block(args):
        qi, bi = args
        qpos = bi * Q_BLOCK + jnp.arange(Q_BLOCK)
        s = jnp.einsum('bhcqd,bhckd->bhcqk', qi, k).astype(jnp.float32) * scale
        s = jnp.where(kpos[None, :] <= qpos[:, None], s, -jnp.inf)
        p = jax.nn.softmax(s, axis=-1)
        a = p[:, :, 0] - lam * p[:, :, 1]
        return jnp.einsum('bhqk,bhkv->bhqv', a.astype(v.dtype), v)

    o = lax.map(block, (qb, jnp.arange(nb)))
    return jnp.moveaxis(o, 0, 2).reshape(B, H, S, v.shape[-1])


def retention(q, k, v):
    B, H, S, dk = q.shape
    dv = v.shape[-1]
    nc = S // CHUNK
    qc = q.astype(jnp.float32).reshape(B, H, nc, CHUNK, dk)
    kc = k.astype(jnp.float32).reshape(B, H, nc, CHUNK, dk)
    vc = v.astype(jnp.float32).reshape(B, H, nc, CHUNK, dv)
    lg = jnp.log(1.0 - 2.0 ** (-5.0 - jnp.arange(H, dtype=jnp.float32)))
    idx = jnp.arange(CHUNK, dtype=jnp.float32)
    rel = idx[:, None] - idx[None, :]
    dmat = jnp.where(rel >= 0, jnp.exp(jnp.maximum(rel, 0.0) * lg[:, None, None]), 0.0)
    inner = jnp.einsum('bhnik,bhnjk->bhnij', qc, kc) * dmat[None, :, None]
    o_in = jnp.einsum('bhnij,bhnjv->bhniv', inner, vc)
    zeta = jnp.exp((CHUNK - 1 - idx)[None, :] * lg[:, None])
    xi = jnp.exp((idx + 1.0)[None, :] * lg[:, None])
    kv = jnp.einsum('bhnjk,bhnjv->nbhkv', kc * zeta[None, :, None, :, None], vc)
    cdecay = jnp.exp(CHUNK * lg)[None, :, None, None]

    def step(state, kv_n):
        return state * cdecay + kv_n, state

    _, s_prev = lax.scan(step, jnp.zeros((B, H, dk, dv), jnp.float32), kv)
    o_x = jnp.einsum('bhnik,nbhkv->bhniv', qc * xi[None, :, None, :, None], s_prev)
    return (o_in + o_x).reshape(B, H, S, dv).astype(v.dtype)


def conformer_conv(a, b, w, bias, ln_g, ln_b, pw_w, pw_b):
    u = a * jax.nn.sigmoid(b)
    K = w.shape[0]
    y = lax.conv_general_dilated(u, w[:, None, :].astype(u.dtype), window_strides=(1,),
                                 padding=[(K - 1, 0)],
                                 dimension_numbers=('NWC', 'WIO', 'NWC'),
                                 feature_group_count=u.shape[-1]) + bias
    y = jax.nn.silu(layer_norm(y, ln_g, ln_b))
    return y @ pw_w + pw_b


def setup_inputs(seed: int = 0) -> dict:
    key = jax.random.key(seed)
    ks = jax.random.split(key, 16)
    f32 = jnp.float32
    nrm = lambda k, shp, s: jax.random.normal(k, shp, f32) * s
    return {
        'x': nrm(ks[0], (BATCH, SEQ, D_MODEL), 1.0),
        'pre_norm_g': 1.0 + nrm(ks[1], (DEPTH, D_MODEL), 0.02),
        'w_in': nrm(ks[2], (DEPTH, D_MODEL, IN_COLS), D_MODEL ** -0.5),
        'lam_q1': nrm(ks[3], (DEPTH, DA_HEAD), 0.1),
        'lam_k1': nrm(ks[4], (DEPTH, DA_HEAD), 0.1),
        'lam_q2': nrm(ks[5], (DEPTH, DA_HEAD), 0.1),
        'lam_k2': nrm(ks[6], (DEPTH, DA_HEAD), 0.1),
        'diff_subln_g': 1.0 + nrm(ks[7], (DEPTH, 2 * DA_HEAD), 0.02),
        'conv_w': nrm(ks[8], (DEPTH, CONV_K, CONV_CH), CONV_K ** -0.5),
        'conv_b': nrm(ks[9], (DEPTH, CONV_CH), 0.01),
        'conv_ln_g': 1.0 + nrm(ks[10], (DEPTH, CONV_CH), 0.02),
        'conv_ln_b': nrm(ks[11], (DEPTH, CONV_CH), 0.01),
        'conv_pw_w': nrm(ks[12], (DEPTH, CONV_CH, CONV_CH), CONV_CH ** -0.5),
        'conv_pw_b': nrm(ks[13], (DEPTH, CONV_CH), 0.01),
        'w_out': nrm(ks[14], (DEPTH, D_MIX, D_MODEL), D_MIX ** -0.5),
        'post_norm_g': 1.0 + nrm(ks[15], (DEPTH, D_MODEL), 0.02),
    }


def reference(x, pre_norm_g, w_in, lam_q1, lam_k1, lam_q2, lam_k2, diff_subln_g,
              conv_w, conv_b, conv_ln_g, conv_ln_b, conv_pw_w, conv_pw_b,
              w_out, post_norm_g):
    B, S, _ = x.shape
    widths = [DA_WIDTH] * 4 + [RET_QK_WIDTH] * 2 + [RET_WIDTH] * 2 + [CONV_CH] * 3
    cuts = np.cumsum(widths)[:-1].tolist()
    for i in range(DEPTH):
        h = rms_norm(x, pre_norm_g[i])
        proj = h @ w_in[i]
        (qa, ka, va, ga, qr, kr, vr, gr, ca, cb, gc) = jnp.split(proj, cuts, axis=-1)

        qa = rope(qa.reshape(B, S, DA_HEADS, 2, DA_HEAD).transpose(0, 2, 3, 1, 4), ROPE_DIM, ROPE_THETA)
        ka = rope(ka.reshape(B, S, DA_HEADS, 2, DA_HEAD).transpose(0, 2, 3, 1, 4), ROPE_DIM, ROPE_THETA)
        va = va.reshape(B, S, DA_HEADS, 2 * DA_HEAD).transpose(0, 2, 1, 3)
        lam_init = 0.8 - 0.6 * math.exp(-0.3 * i)
        lam = (jnp.exp(jnp.sum(lam_q1[i].astype(jnp.float32) * lam_k1[i].astype(jnp.float32)))
               - jnp.exp(jnp.sum(lam_q2[i].astype(jnp.float32) * lam_k2[i].astype(jnp.float32)))
               + lam_init)
        oa = diff_attention(qa, ka, va, lam)
        oa = rms_norm(oa, diff_subln_g[i]) * (1.0 - lam_init)
        ya = oa.transpose(0, 2, 1, 3).reshape(B, S, DA_WIDTH) * jax.nn.silu(ga)

        qr = rope(qr.reshape(B, S, RET_HEADS, RET_QK_DIM).transpose(0, 2, 1, 3), RET_QK_DIM, RET_THETA)
        kr = rope(kr.reshape(B, S, RET_HEADS, RET_QK_DIM).transpose(0, 2, 1, 3), RET_QK_DIM, RET_THETA) * (RET_QK_DIM ** -0.5)
        vr = vr.reshape(B, S, RET_HEADS, RET_V_DIM).transpose(0, 2, 1, 3)
        orr = rms_norm(retention(qr, kr, vr))
        yr = orr.transpose(0, 2, 1, 3).reshape(B, S, RET_WIDTH) * jax.nn.silu(gr)

        yc = conformer_conv(ca, cb, conv_w[i], conv_b[i], conv_ln_g[i], conv_ln_b[i],
                            conv_pw_w[i], conv_pw_b[i]) * jax.nn.silu(gc)

        mix = jnp.concatenate([ya, yr, yc], axis=-1) @ w_out[i]
        x = x + rms_norm(mix, post_norm_g[i])
    return x
```

```python
import functools
import math

import jax
import jax.numpy as jnp
from jax import lax
from jax.experimental import pallas as pl
from jax.experimental.pallas import tpu as pltpu

D_MODEL = 4096
SEQ = 8192
DEPTH = 4
DA_WIDTH = 2048
DA_HEAD = 128
DA_HEADS = 8
DA_VDIM = 2 * DA_HEAD
ROPE_DIM = 32
ROPE_THETA = 500000.0
RET_WIDTH = 1024
RET_HEADS = 4
RET_V_DIM = 256
RET_QK_DIM = 128
RET_QK_WIDTH = 512
RET_THETA = 10000.0
CONV_CH = 1024
CONV_K = 31
CHUNK = 128
IN_COLS = 14336
EPS = 1e-6
LN_EPS = 1e-5

OFF_QA = 0
OFF_KA = 2048
OFF_VA = 4096
OFF_GA = 6144
OFF_QR = 8192
OFF_KR = 8704
OFF_VR = 9216
OFF_GR = 10240
OFF_CA = 11264
OFF_CB = 12288
OFF_GC = 13312

LANES = 128

PRE_TM = 256
INP_TM = 1024
INP_TN = 1024
ATT_T = 512
RET_TM = 1024
CONV_T = 256
CONV_HALO = 32
CONV_R = 32
OUT_TM = 1024
OUT_TN = 1024
VMEM_LIMIT = 56 * 1024 * 1024

F32 = jnp.float32
BF16 = jnp.bfloat16


def _cparams(sem):
    return pltpu.CompilerParams(dimension_semantics=sem, vmem_limit_bytes=VMEM_LIMIT)


def _silu(x):
    return x * jax.nn.sigmoid(x)


def _prenorm_kernel(x_ref, g_ref, h_ref):
    x = x_ref[...]
    r = lax.rsqrt(jnp.mean(x * x, axis=-1, keepdims=True) + EPS)
    h_ref[...] = (x * r * g_ref[...]).astype(BF16)


def prenorm(x, g):
    S, D = x.shape
    return pl.pallas_call(
        _prenorm_kernel,
        grid=(S // PRE_TM,),
        in_specs=[pl.BlockSpec((PRE_TM, D), lambda i: (i, 0)),
                  pl.BlockSpec((1, D), lambda i: (0, 0))],
        out_specs=pl.BlockSpec((PRE_TM, D), lambda i: (i, 0)),
        out_shape=jax.ShapeDtypeStruct((S, D), BF16),
        compiler_params=_cparams(("arbitrary",)),
        name="prenorm",
    )(x, g.reshape(1, D))


def _post_kernel(mix_ref, x_ref, gp_ref, gn_ref, xo_ref, h_ref):
    mix = mix_ref[...]
    r = lax.rsqrt(jnp.mean(mix * mix, axis=-1, keepdims=True) + EPS)
    xn = x_ref[...] + mix * r * gp_ref[...]
    xo_ref[...] = xn
    r2 = lax.rsqrt(jnp.mean(xn * xn, axis=-1, keepdims=True) + EPS)
    h_ref[...] = (xn * r2 * gn_ref[...]).astype(BF16)


def post_norm_residual(mix, x, g_post, g_next):
    S, D = x.shape
    row = pl.BlockSpec((PRE_TM, D), lambda i: (i, 0))
    vec = pl.BlockSpec((1, D), lambda i: (0, 0))
    return pl.pallas_call(
        _post_kernel,
        grid=(S // PRE_TM,),
        in_specs=[row, row, vec, vec],
        out_specs=[row, row],
        out_shape=[jax.ShapeDtypeStruct((S, D), F32), jax.ShapeDtypeStruct((S, D), BF16)],
        compiler_params=_cparams(("arbitrary",)),
        name="post_norm_residual",
    )(mix, x, g_post.reshape(1, D), g_next.reshape(1, D))


def _inproj_kernel(h_ref, w_ref, ca_ref, sa_ref, sb_ref, cr_ref, sr_ref, o_ref):
    j = pl.program_id(1)
    acc = jnp.dot(h_ref[...], w_ref[...], preferred_element_type=F32)
    n_groups = INP_TN // LANES
    qa_blocks = DA_WIDTH // INP_TN
    qk_blocks = 2 * DA_WIDTH // INP_TN
    ret_block = OFF_QR // INP_TN

    @pl.when(j < qk_blocks)
    def _():
        sc = jnp.where(j < qa_blocks, DA_HEAD ** -0.5, 1.0).astype(F32)
        c, sa, sb = ca_ref[...], sa_ref[...], sb_ref[...]
        for g in range(n_groups):
            xg = acc[:, g * LANES:(g + 1) * LANES]
            y = (xg * c + pltpu.roll(xg, LANES - ROPE_DIM // 2, 1) * sa
                 + pltpu.roll(xg, ROPE_DIM // 2, 1) * sb)
            o_ref[:, g * LANES:(g + 1) * LANES] = (y * sc).astype(BF16)

    @pl.when(j == ret_block)
    def _():
        c, s = cr_ref[...], sr_ref[...]
        for g in range(n_groups):
            xg = acc[:, g * LANES:(g + 1) * LANES]
            y = xg * c + pltpu.roll(xg, RET_QK_DIM // 2, 1) * s
            if g >= RET_QK_WIDTH // LANES:
                y = y * (RET_QK_DIM ** -0.5)
            o_ref[:, g * LANES:(g + 1) * LANES] = y.astype(BF16)

    @pl.when(jnp.logical_and(j >= qk_blocks, j != ret_block))
    def _():
        o_ref[...] = acc.astype(BF16)


def in_projection(h, w, tabs):
    S, D = h.shape
    N = w.shape[1]
    assert OFF_QR % INP_TN == 0 and 2 * RET_QK_WIDTH == INP_TN
    tab = pl.BlockSpec((INP_TM, LANES), lambda i, j: (i, 0))
    return pl.pallas_call(
        _inproj_kernel,
        grid=(S // INP_TM, N // INP_TN),
        in_specs=[pl.BlockSpec((INP_TM, D), lambda i, j: (i, 0)),
                  pl.BlockSpec((D, INP_TN), lambda i, j: (0, j)),
                  tab, tab, tab, tab, tab],
        out_specs=pl.BlockSpec((INP_TM, INP_TN), lambda i, j: (i, j)),
        out_shape=jax.ShapeDtypeStruct((S, N), BF16),
        compiler_params=_cparams(("arbitrary", "arbitrary")),
        name="in_projection",
    )(h, w, *tabs)


def rope_tables(S):
    pos = jnp.arange(S, dtype=F32)
    lane = jnp.arange(LANES)
    half = ROPE_DIM // 2
    inv = 1.0 / (ROPE_THETA ** (jnp.arange(0, ROPE_DIM, 2, dtype=F32) / ROPE_DIM))
    ang = pos[:, None] * inv[None, :]
    cos, sin = jnp.cos(ang), jnp.sin(ang)
    ones = jnp.ones((S, LANES - ROPE_DIM), F32)
    zeros = jnp.zeros((S, LANES - half), F32)
    ca = jnp.concatenate([cos, cos, ones], axis=1)
    sa = jnp.concatenate([-sin, zeros], axis=1)
    sb = jnp.concatenate([jnp.zeros((S, half), F32), sin,
                          jnp.zeros((S, LANES - ROPE_DIM), F32)], axis=1)
    inv_r = 1.0 / (RET_THETA ** (jnp.arange(0, RET_QK_DIM, 2, dtype=F32) / RET_QK_DIM))
    ang_r = pos[:, None] * inv_r[None, :]
    cos_r, sin_r = jnp.cos(ang_r), jnp.sin(ang_r)
    cr = jnp.concatenate([cos_r, cos_r], axis=1)
    sr = jnp.concatenate([-sin_r, sin_r], axis=1)
    del lane
    return ca, sa, sb, cr, sr


def _attn_kernel(lam_init, q_ref, k_ref, v_ref, g_ref, lq1_ref, lk1_ref, lq2_ref, lk2_ref,
                 sg_ref, o_ref, m_ref, l_ref, acc_ref):
    T = ATT_T
    qi = pl.program_id(1)
    m_ref[...] = jnp.full(m_ref.shape, -jnp.inf, F32)
    l_ref[...] = jnp.zeros(l_ref.shape, F32)
    acc_ref[...] = jnp.zeros(acc_ref.shape, F32)

    def kv_step(kb, masked):
        start = pl.multiple_of(kb * T, T)
        v = v_ref[pl.ds(start, T), :]
        for c in range(2):
            q = q_ref[:, c * DA_HEAD:(c + 1) * DA_HEAD]
            k = k_ref[pl.ds(start, T), c * DA_HEAD:(c + 1) * DA_HEAD]
            s = lax.dot_general(q, k, (((1,), (1,)), ((), ())), preferred_element_type=F32)
            if masked:
                row = lax.broadcasted_iota(jnp.int32, (T, T), 0)
                col = lax.broadcasted_iota(jnp.int32, (T, T), 1)
                s = jnp.where(col <= row, s, -jnp.inf)
            m_prev = m_ref[c]
            m_new = jnp.maximum(m_prev, jnp.max(s, axis=-1, keepdims=True))
            alpha = jnp.exp(m_prev - m_new)
            p = jnp.exp(s - m_new)
            l_ref[c] = alpha * l_ref[c] + jnp.sum(p, axis=-1, keepdims=True)
            acc_ref[c] = alpha * acc_ref[c] + jnp.dot(p.astype(BF16), v,
                                                      preferred_element_type=F32)
            m_ref[c] = m_new

    def body(kb, carry):
        kv_step(kb, False)
        return carry

    lax.fori_loop(0, qi, body, 0)
    kv_step(qi, True)

    lam = (jnp.exp(jnp.sum(lq1_ref[...] * lk1_ref[...], axis=-1, keepdims=True))
           - jnp.exp(jnp.sum(lq2_ref[...] * lk2_ref[...], axis=-1, keepdims=True))
           + lam_init)
    o = acc_ref[0] * (1.0 / l_ref[0]) - lam * (acc_ref[1] * (1.0 / l_ref[1]))
    r = lax.rsqrt(jnp.mean(o * o, axis=-1, keepdims=True) + EPS)
    y = o * r * sg_ref[...] * (1.0 - lam_init)
    o_ref[...] = (y * _silu(g_ref[...].astype(F32))).astype(BF16)


def diff_attention(proj, lq1, lk1, lq2, lk2, subln_g, lam_init):
    S = proj.shape[0]
    T = ATT_T
    nq = S // T
    qb, kb, vb, gb = (OFF_QA // DA_VDIM, OFF_KA // DA_VDIM, OFF_VA // DA_VDIM, OFF_GA // DA_VDIM)
    vec = pl.BlockSpec((1, DA_HEAD), lambda h, i: (0, 0))
    return pl.pallas_call(
        functools.partial(_attn_kernel, lam_init),
        grid=(DA_HEADS, nq),
        in_specs=[pl.BlockSpec((T, DA_VDIM), lambda h, i: (i, qb + h)),
                  pl.BlockSpec((S, DA_VDIM), lambda h, i: (0, kb + h)),
                  pl.BlockSpec((S, DA_VDIM), lambda h, i: (0, vb + h)),
                  pl.BlockSpec((T, DA_VDIM), lambda h, i: (i, gb + h)),
                  vec, vec, vec, vec,
                  pl.BlockSpec((1, DA_VDIM), lambda h, i: (0, 0))],
        out_specs=pl.BlockSpec((T, DA_VDIM), lambda h, i: (i, h)),
        out_shape=jax.ShapeDtypeStruct((S, DA_WIDTH), BF16),
        scratch_shapes=[pltpu.VMEM((2, T, 1), F32), pltpu.VMEM((2, T, 1), F32),
                        pltpu.VMEM((2, T, DA_VDIM), F32)],
        compiler_params=_cparams(("arbitrary", "arbitrary")),
        name="diff_attention",
    )(proj, proj, proj, proj, lq1.reshape(1, -1), lk1.reshape(1, -1), lq2.reshape(1, -1),
      lk2.reshape(1, -1), subln_g.reshape(1, -1))


def _ret_kernel(q_ref, k_ref, v_ref, g_ref, dm_ref, zeta_ref, xi_ref, cd_ref, o_ref, st_ref):
    @pl.when(pl.program_id(1) == 0)
    def _():
        st_ref[...] = jnp.zeros(st_ref.shape, F32)

    dmat = dm_ref[0]
    zeta = zeta_ref[0]
    xi = xi_ref[0]
    cdecay = cd_ref[0]
    for n in range(RET_TM // CHUNK):
        rows = slice(n * CHUNK, (n + 1) * CHUNK)
        qc = q_ref[rows, :]
        kc = k_ref[rows, :]
        vc = v_ref[rows, :]
        state = st_ref[...]
        inner = lax.dot_general(qc, kc, (((1,), (1,)), ((), ())),
                                preferred_element_type=F32) * dmat
        o = jnp.dot(inner.astype(BF16), vc, preferred_element_type=F32)
        o = o + jnp.dot((qc.astype(F32) * xi).astype(BF16), state.astype(BF16),
                        preferred_element_type=F32)
        kz = (kc.astype(F32) * zeta).astype(BF16)
        st_ref[...] = state * cdecay + lax.dot_general(
            kz, vc, (((0,), (0,)), ((), ())), preferred_element_type=F32)
        r = lax.rsqrt(jnp.mean(o * o, axis=-1, keepdims=True) + EPS)
        o_ref[rows, :] = (o * r * _silu(g_ref[rows, :].astype(F32))).astype(BF16)


def retention_tables():
    lg = jnp.log(1.0 - 2.0 ** (-5.0 - jnp.arange(RET_HEADS, dtype=F32)))
    idx = jnp.arange(CHUNK, dtype=F32)
    rel = idx[:, None] - idx[None, :]
    dmat = jnp.where(rel >= 0, jnp.exp(jnp.maximum(rel, 0.0) * lg[:, None, None]), 0.0)
    zeta = jnp.exp((CHUNK - 1 - idx)[None, :] * lg[:, None])[..., None]
    xi = jnp.exp((idx + 1.0)[None, :] * lg[:, None])[..., None]
    cdecay = jnp.exp(CHUNK * lg).reshape(RET_HEADS, 1, 1)
    return dmat, zeta, xi, cdecay


def retention(proj, tabs):
    S = proj.shape[0]
    qb, kb = OFF_QR // RET_QK_DIM, OFF_KR // RET_QK_DIM
    vb, gb = OFF_VR // RET_V_DIM, OFF_GR // RET_V_DIM
    dmat, zeta, xi, cdecay = tabs
    return pl.pallas_call(
        _ret_kernel,
        grid=(RET_HEADS, S // RET_TM),
        in_specs=[pl.BlockSpec((RET_TM, RET_QK_DIM), lambda h, i: (i, qb + h)),
                  pl.BlockSpec((RET_TM, RET_QK_DIM), lambda h, i: (i, kb + h)),
                  pl.BlockSpec((RET_TM, RET_V_DIM), lambda h, i: (i, vb + h)),
                  pl.BlockSpec((RET_TM, RET_V_DIM), lambda h, i: (i, gb + h)),
                  pl.BlockSpec((1, CHUNK, CHUNK), lambda h, i: (h, 0, 0)),
                  pl.BlockSpec((1, CHUNK, 1), lambda h, i: (h, 0, 0)),
                  pl.BlockSpec((1, CHUNK, 1), lambda h, i: (h, 0, 0)),
                  pl.BlockSpec((1, 1, 1), lambda h, i: (h, 0, 0))],
        out_specs=pl.BlockSpec((RET_TM, RET_V_DIM), lambda h, i: (i, h)),
        out_shape=jax.ShapeDtypeStruct((S, RET_WIDTH), BF16),
        scratch_shapes=[pltpu.VMEM((RET_QK_DIM, RET_V_DIM), F32)],
        compiler_params=_cparams(("arbitrary", "arbitrary")),
        name="retention",
    )(proj, proj, proj, proj, dmat, zeta, xi, cdecay)


def _conv_kernel(a_ref, b_ref, ha_ref, hb_ref, g_ref, w_ref, cb_ref, lg_ref, lb_ref,
                 pw_ref, pb_ref, o_ref, u_ref, z_ref):
    i = pl.program_id(0)
    T, H, R = CONV_T, CONV_HALO, CONV_R
    keep = (i > 0).astype(F32)
    u_ref[0:H, :] = ha_ref[...].astype(F32) * jax.nn.sigmoid(hb_ref[...].astype(F32)) * keep
    u_ref[H:H + T, :] = a_ref[...].astype(F32) * jax.nn.sigmoid(b_ref[...].astype(F32))
    first = H - (CONV_K - 1)
    for r in range(T // R):
        acc = jnp.broadcast_to(cb_ref[...], (R, CONV_CH))
        for j in range(CONV_K):
            lo = r * R + first + j
            acc = acc + w_ref[j:j + 1, :] * u_ref[lo:lo + R, :]
        mu = jnp.mean(acc, axis=-1, keepdims=True)
        d = acc - mu
        var = jnp.mean(d * d, axis=-1, keepdims=True)
        yn = d * lax.rsqrt(var + LN_EPS) * lg_ref[...] + lb_ref[...]
        z_ref[r * R:(r + 1) * R, :] = _silu(yn).astype(BF16)
    out = jnp.dot(z_ref[...], pw_ref[...], preferred_element_type=F32) + pb_ref[...]
    o_ref[...] = (out * _silu(g_ref[...].astype(F32))).astype(BF16)


def conformer_conv(proj, conv_w, conv_b, ln_g, ln_b, pw_w, pw_b):
    S = proj.shape[0]
    T, H = CONV_T, CONV_HALO
    ab, bb, gb = OFF_CA // CONV_CH, OFF_CB // CONV_CH, OFF_GC // CONV_CH
    hpb = T // H
    vec = pl.BlockSpec((1, CONV_CH), lambda i: (0, 0))
    return pl.pallas_call(
        _conv_kernel,
        grid=(S // T,),
        in_specs=[pl.BlockSpec((T, CONV_CH), lambda i: (i, ab)),
                  pl.BlockSpec((T, CONV_CH), lambda i: (i, bb)),
                  pl.BlockSpec((H, CONV_CH), lambda i: (jnp.maximum(i * hpb - 1, 0), ab)),
                  pl.BlockSpec((H, CONV_CH), lambda i: (jnp.maximum(i * hpb - 1, 0), bb)),
                  pl.BlockSpec((T, CONV_CH), lambda i: (i, gb)),
                  pl.BlockSpec((CONV_K, CONV_CH), lambda i: (0, 0)),
                  vec, vec, vec,
                  pl.BlockSpec((CONV_CH, CONV_CH), lambda i: (0, 0)),
                  vec],
        out_specs=pl.BlockSpec((T, CONV_CH), lambda i: (i, 0)),
        out_shape=jax.ShapeDtypeStruct((S, CONV_CH), BF16),
        scratch_shapes=[pltpu.VMEM((T + H, CONV_CH), F32), pltpu.VMEM((T, CONV_CH), BF16)],
        compiler_params=_cparams(("arbitrary",)),
        name="conformer_conv",
    )(proj, proj, proj, proj, proj, conv_w, conv_b.reshape(1, -1), ln_g.reshape(1, -1),
      ln_b.reshape(1, -1), pw_w, pw_b.reshape(1, -1))


def _outproj_kernel(ya_ref, yr_ref, yc_ref, w_ref, o_ref):
    a_end = DA_WIDTH
    r_end = DA_WIDTH + RET_WIDTH
    acc = jnp.dot(ya_ref[...], w_ref[0:a_end, :], preferred_element_type=F32)
    acc = acc + jnp.dot(yr_ref[...], w_ref[a_end:r_end, :], preferred_element_type=F32)
    acc = acc + jnp.dot(yc_ref[...], w_ref[r_end:, :], preferred_element_type=F32)
    o_ref[...] = acc


def out_projection(ya, yr, yc, w):
    S = ya.shape[0]
    K, N = w.shape
    return pl.pallas_call(
        _outproj_kernel,
        grid=(S // OUT_TM, N // OUT_TN),
        in_specs=[pl.BlockSpec((OUT_TM, DA_WIDTH), lambda i, j: (i, 0)),
                  pl.BlockSpec((OUT_TM, RET_WIDTH), lambda i, j: (i, 0)),
                  pl.BlockSpec((OUT_TM, CONV_CH), lambda i, j: (i, 0)),
                  pl.BlockSpec((K, OUT_TN), lambda i, j: (0, j))],
        out_specs=pl.BlockSpec((OUT_TM, OUT_TN), lambda i, j: (i, j)),
        out_shape=jax.ShapeDtypeStruct((S, N), F32),
        compiler_params=_cparams(("arbitrary", "arbitrary")),
        name="out_projection",
    )(ya, yr, yc, w)


def kernel(x, pre_norm_g, w_in, lam_q1, lam_k1, lam_q2, lam_k2, diff_subln_g, conv_w, conv_b,
           conv_ln_g, conv_ln_b, conv_pw_w, conv_pw_b, w_out, post_norm_g):
    B, S, D = x.shape
    assert (B, S, D) == (1, SEQ, D_MODEL)
    xs = x.reshape(S, D)
    rtabs = rope_tables(S)
    ret_tabs = retention_tables()
    h = prenorm(xs, pre_norm_g[0])
    for i in range(DEPTH):
        lam_init = 0.8 - 0.6 * math.exp(-0.3 * i)
        proj = in_projection(h, w_in[i].astype(BF16), rtabs)
        ya = diff_attention(proj, lam_q1[i], lam_k1[i], lam_q2[i], lam_k2[i],
                            diff_subln_g[i], lam_init)
        yr = retention(proj, ret_tabs)
        yc = conformer_conv(proj, conv_w[i], conv_b[i], conv_ln_g[i], conv_ln_b[i],
                            conv_pw_w[i].astype(BF16), conv_pw_b[i])
        mix = out_projection(ya, yr, yc, w_out[i].astype(BF16))
        g_next = pre_norm_g[(i + 1) % DEPTH]
        xs, h = post_norm_residual(mix, xs, post_norm_g[i], g_next)
    return xs.reshape(B, S, D)
```

```python
import functools
import math

import jax
import jax.numpy as jnp
from jax import lax
from jax.experimental import pallas as pl
from jax.experimental.pallas import tpu as pltpu

D_MODEL = 4096
SEQ = 8192
DEPTH = 4
DA_WIDTH = 2048
DA_HEAD = 128
DA_HEADS = 8
DA_VDIM = 2 * DA_HEAD
ROPE_DIM = 32
ROPE_THETA = 500000.0
RET_WIDTH = 1024
RET_HEADS = 4
RET_V_DIM = 256
RET_QK_DIM = 128
RET_QK_WIDTH = 512
RET_THETA = 10000.0
CONV_CH = 1024
CONV_K = 31
CHUNK = 128
IN_COLS = 14336
EPS = 1e-6
LN_EPS = 1e-5

OFF_QA = 0
OFF_KA = 2048
OFF_VA = 4096
OFF_GA = 6144
OFF_QR = 8192
OFF_KR = 8704
OFF_VR = 9216
OFF_GR = 10240
OFF_CA = 11264
OFF_CB = 12288
OFF_GC = 13312

LANES = 128
SUBLANES = 8

PRE_TM = 256
INP_TM = 1024
INP_TN = 1024
ATT_T = 512
ATT_TQ = 1024
RET_TM = 1024
CONV_T = 256
CONV_HALO = 32
OUT_TM = 1024
OUT_TN = 1024
VMEM_LIMIT = 56 * 1024 * 1024
LOG2E = math.log2(math.e)

F32 = jnp.float32
BF16 = jnp.bfloat16


def _cparams(sem):
    return pltpu.CompilerParams(dimension_semantics=sem, vmem_limit_bytes=VMEM_LIMIT)


def _silu(x):
    return x * jax.nn.sigmoid(x)


def _prenorm_kernel(x_ref, g_ref, h_ref):
    x = x_ref[...]
    r = lax.rsqrt(jnp.mean(x * x, axis=-1, keepdims=True) + EPS)
    h_ref[...] = (x * r * g_ref[...]).astype(BF16)


def prenorm(x, g):
    S, D = x.shape
    return pl.pallas_call(
        _prenorm_kernel,
        grid=(S // PRE_TM,),
        in_specs=[pl.BlockSpec((PRE_TM, D), lambda i: (i, 0)),
                  pl.BlockSpec((1, D), lambda i: (0, 0))],
        out_specs=pl.BlockSpec((PRE_TM, D), lambda i: (i, 0)),
        out_shape=jax.ShapeDtypeStruct((S, D), BF16),
        compiler_params=_cparams(("arbitrary",)),
        name="prenorm",
    )(x, g.reshape(1, D))


def _post_kernel(mix_ref, x_ref, gp_ref, gn_ref, xo_ref, h_ref):
    mix = mix_ref[...]
    r = lax.rsqrt(jnp.mean(mix * mix, axis=-1, keepdims=True) + EPS)
    xn = x_ref[...] + mix * r * gp_ref[...]
    xo_ref[...] = xn
    r2 = lax.rsqrt(jnp.mean(xn * xn, axis=-1, keepdims=True) + EPS)
    h_ref[...] = (xn * r2 * gn_ref[...]).astype(BF16)


def post_norm_residual(mix, x, g_post, g_next):
    S, D = x.shape
    row = pl.BlockSpec((PRE_TM, D), lambda i: (i, 0))
    vec = pl.BlockSpec((1, D), lambda i: (0, 0))
    return pl.pallas_call(
        _post_kernel,
        grid=(S // PRE_TM,),
        in_specs=[row, row, vec, vec],
        out_specs=[row, row],
        out_shape=[jax.ShapeDtypeStruct((S, D), F32), jax.ShapeDtypeStruct((S, D), BF16)],
        compiler_params=_cparams(("arbitrary",)),
        name="post_norm_residual",
    )(mix, x, g_post.reshape(1, D), g_next.reshape(1, D))


def _inproj_kernel(h_ref, w_ref, ca_ref, sa_ref, sb_ref, cr_ref, sr_ref, o_ref, vt_ref):
    j = pl.program_id(1)
    acc = jnp.dot(h_ref[...], w_ref[...], preferred_element_type=F32)
    n_groups = INP_TN // LANES
    qa_blocks = DA_WIDTH // INP_TN
    qk_blocks = 2 * DA_WIDTH // INP_TN
    va_first = OFF_VA // INP_TN
    ret_block = OFF_QR // INP_TN

    @pl.when(jnp.logical_and(j >= va_first, j < va_first + qa_blocks))
    def _():
        acc_t = acc.T
        for b in range(INP_TM // ATT_T):
            vt_ref[b] = acc_t[:, b * ATT_T:(b + 1) * ATT_T].astype(BF16)

    @pl.when(j < qk_blocks)
    def _():
        sc = jnp.where(j < qa_blocks, DA_HEAD ** -0.5 * LOG2E, 1.0).astype(F32)
        c, sa, sb = ca_ref[...], sa_ref[...], sb_ref[...]
        for g in range(n_groups):
            xg = acc[:, g * LANES:(g + 1) * LANES]
            y = (xg * c + pltpu.roll(xg, LANES - ROPE_DIM // 2, 1) * sa
                 + pltpu.roll(xg, ROPE_DIM // 2, 1) * sb)
            o_ref[:, g * LANES:(g + 1) * LANES] = (y * sc).astype(BF16)

    @pl.when(j == ret_block)
    def _():
        c, s = cr_ref[...], sr_ref[...]
        for g in range(n_groups):
            xg = acc[:, g * LANES:(g + 1) * LANES]
            y = xg * c + pltpu.roll(xg, RET_QK_DIM // 2, 1) * s
            if g >= RET_QK_WIDTH // LANES:
                y = y * (RET_QK_DIM ** -0.5)
            o_ref[:, g * LANES:(g + 1) * LANES] = y.astype(BF16)

    @pl.when(jnp.logical_and(j >= qk_blocks, j != ret_block))
    def _():
        o_ref[...] = acc.astype(BF16)


def in_projection(h, w, tabs):
    S, D = h.shape
    N = w.shape[1]
    assert OFF_QR % INP_TN == 0 and 2 * RET_QK_WIDTH == INP_TN
    tab = pl.BlockSpec((INP_TM, LANES), lambda i, j: (i, 0))
    va_first = OFF_VA // INP_TN
    va_last = va_first + DA_WIDTH // INP_TN - 1
    kb_per_tile = INP_TM // ATT_T
    vt_spec = pl.BlockSpec((kb_per_tile, INP_TN, ATT_T),
                           lambda i, j: (i, jnp.clip(j, va_first, va_last) - va_first, 0))
    return pl.pallas_call(
        _inproj_kernel,
        grid=(S // INP_TM, N // INP_TN),
        in_specs=[pl.BlockSpec((INP_TM, D), lambda i, j: (i, 0)),
                  pl.BlockSpec((D, INP_TN), lambda i, j: (0, j)),
                  tab, tab, tab, tab, tab],
        out_specs=[pl.BlockSpec((INP_TM, INP_TN), lambda i, j: (i, j)), vt_spec],
        out_shape=[jax.ShapeDtypeStruct((S, N), BF16),
                   jax.ShapeDtypeStruct((S // ATT_T, DA_WIDTH, ATT_T), BF16)],
        compiler_params=_cparams(("arbitrary", "arbitrary")),
        name="in_projection",
    )(h, w, *tabs)


def rope_tables(S):
    pos = jnp.arange(S, dtype=F32)
    lane = jnp.arange(LANES)
    half = ROPE_DIM // 2
    inv = 1.0 / (ROPE_THETA ** (jnp.arange(0, ROPE_DIM, 2, dtype=F32) / ROPE_DIM))
    ang = pos[:, None] * inv[None, :]
    cos, sin = jnp.cos(ang), jnp.sin(ang)
    ones = jnp.ones((S, LANES - ROPE_DIM), F32)
    zeros = jnp.zeros((S, LANES - half), F32)
    ca = jnp.concatenate([cos, cos, ones], axis=1)
    sa = jnp.concatenate([-sin, zeros], axis=1)
    sb = jnp.concatenate([jnp.zeros((S, half), F32), sin,
                          jnp.zeros((S, LANES - ROPE_DIM), F32)], axis=1)
    inv_r = 1.0 / (RET_THETA ** (jnp.arange(0, RET_QK_DIM, 2, dtype=F32) / RET_QK_DIM))
    ang_r = pos[:, None] * inv_r[None, :]
    cos_r, sin_r = jnp.cos(ang_r), jnp.sin(ang_r)
    cr = jnp.concatenate([cos_r, cos_r], axis=1)
    sr = jnp.concatenate([-sin_r, sin_r], axis=1)
    del lane
    return ca, sa, sb, cr, sr


def _attn_kernel(lam_init, q_ref, k_ref, vt_ref, g_ref, lq1_ref, lk1_ref, lq2_ref, lk2_ref,
                 sg_ref, o_ref, acc_ref, sa_ref, sb_ref):
    TQ, TK = ATT_TQ, ATT_T
    NT = (((1,), (1,)), ((), ()))
    qi = pl.program_id(1)
    acc_ref[...] = jnp.zeros(acc_ref.shape, F32)

    def scores(kb, s_ref, cols, diag):
        start = pl.multiple_of(kb * TK, TK)
        n = cols.stop - cols.start
        mb = []
        for c in range(2):
            k = k_ref[pl.ds(start, TK), c * DA_HEAD:(c + 1) * DA_HEAD]
            q = q_ref[cols, c * DA_HEAD:(c + 1) * DA_HEAD]
            s = lax.dot_general(k, q, NT, preferred_element_type=F32)
            if diag:
                key = lax.broadcasted_iota(jnp.int32, (TK, n), 0)
                qry = lax.broadcasted_iota(jnp.int32, (TK, n), 1)
                s = jnp.where(key <= qry, s, -jnp.inf)
            s_ref[c, :, cols] = s
            mb.append(jnp.max(s, axis=0, keepdims=True))
        return tuple(mb)

    def softmax_pv(kb, s_ref, mb, stats, cols, diag_cols=0):
        vt = vt_ref[kb]
        n = cols.stop - cols.start
        out = []
        for c in range(2):
            m_prev, l_prev = stats[2 * c][:, cols], stats[2 * c + 1][:, cols]
            s = s_ref[c, :, cols]
            m_blk = mb[c] if mb is not None else None
            if diag_cols:
                key = lax.broadcasted_iota(jnp.int32, (TK, n), 0)
                qry = lax.broadcasted_iota(jnp.int32, (TK, n), 1)
                s = jnp.where(jnp.logical_or(key <= qry, qry >= diag_cols), s, -jnp.inf)
                m_blk = jnp.max(s, axis=0, keepdims=True)
            m_new = jnp.maximum(m_prev, m_blk)
            alpha = jnp.exp2(m_prev - m_new)
            p = jnp.exp2(s - m_new)
            l_new = alpha * l_prev + jnp.sum(p, axis=0, keepdims=True)
            acc_ref[c, :, cols] = alpha * acc_ref[c, :, cols] + jnp.dot(
                vt, p.astype(BF16), preferred_element_type=F32)
            if n == TQ:
                out += [m_new, l_new]
            else:
                out += [jnp.concatenate([stats[2 * c][:, :cols.start], m_new], axis=1),
                        jnp.concatenate([stats[2 * c + 1][:, :cols.start], l_new], axis=1)]
        return tuple(out)

    full = slice(0, TQ)
    late = slice(TQ - TK, TQ)
    neg = jnp.full((1, TQ), -jnp.inf, F32)
    zero = jnp.zeros((1, TQ), F32)

    def body(j, carry):
        stats, mb_a = carry[:4], carry[4:]
        mb_b = scores(2 * j + 1, sb_ref, full, False)
        stats = softmax_pv(2 * j, sa_ref, mb_a, stats, full)
        mb_a = scores(2 * j + 2, sa_ref, full, False)
        stats = softmax_pv(2 * j + 1, sb_ref, mb_b, stats, full)
        return stats + mb_a

    mb_a = scores(0, sa_ref, full, False)
    carry = lax.fori_loop(0, qi, body, (neg, zero, neg, zero) + mb_a)
    stats = carry[:4]
    mb_b = scores(2 * qi + 1, sb_ref, late, True)
    stats = softmax_pv(2 * qi, sa_ref, None, stats, full, diag_cols=TK)
    _, l0, _, l1 = softmax_pv(2 * qi + 1, sb_ref, mb_b, stats, late)

    lam = (jnp.exp(jnp.sum(lq1_ref[...] * lk1_ref[...], axis=-1, keepdims=True))
           - jnp.exp(jnp.sum(lq2_ref[...] * lk2_ref[...], axis=-1, keepdims=True))
           + lam_init)
    o_t = acc_ref[0] * (1.0 / l0) - lam * (acc_ref[1] * (1.0 / l1))
    o = o_t.T
    r = lax.rsqrt(jnp.mean(o * o, axis=-1, keepdims=True) + EPS)
    y = o * r * sg_ref[...] * (1.0 - lam_init)
    o_ref[...] = (y * _silu(g_ref[...].astype(F32))).astype(BF16)


def diff_attention(proj, v_t, lq1, lk1, lq2, lk2, subln_g, lam_init):
    S = proj.shape[0]
    TQ, TK = ATT_TQ, ATT_T
    assert TQ == 2 * TK
    qb, kb, gb = OFF_QA // DA_VDIM, OFF_KA // DA_VDIM, OFF_GA // DA_VDIM
    vec = pl.BlockSpec((1, DA_HEAD), lambda h, i: (0, 0))
    return pl.pallas_call(
        functools.partial(_attn_kernel, lam_init),
        grid=(DA_HEADS, S // TQ),
        in_specs=[pl.BlockSpec((TQ, DA_VDIM), lambda h, i: (i, qb + h)),
                  pl.BlockSpec((S, DA_VDIM), lambda h, i: (0, kb + h)),
                  pl.BlockSpec((S // TK, DA_VDIM, TK), lambda h, i: (0, h, 0)),
                  pl.BlockSpec((TQ, DA_VDIM), lambda h, i: (i, gb + h)),
                  vec, vec, vec, vec,
                  pl.BlockSpec((1, DA_VDIM), lambda h, i: (0, 0))],
        out_specs=pl.BlockSpec((TQ, DA_VDIM), lambda h, i: (i, h)),
        out_shape=jax.ShapeDtypeStruct((S, DA_WIDTH), BF16),
        scratch_shapes=[pltpu.VMEM((2, DA_VDIM, TQ), F32),
                        pltpu.VMEM((2, TK, TQ), F32), pltpu.VMEM((2, TK, TQ), F32)],
        compiler_params=_cparams(("arbitrary", "arbitrary")),
        name="diff_attention",
    )(proj, proj, v_t, proj, lq1.reshape(1, -1), lk1.reshape(1, -1), lq2.reshape(1, -1),
      lk2.reshape(1, -1), subln_g.reshape(1, -1))


def _ret_kernel(q_ref, k_ref, v_ref, g_ref, dm_ref, zeta_ref, xi_ref, cd_ref, o_ref, st_ref):
    @pl.when(pl.program_id(1) == 0)
    def _():
        st_ref[...] = jnp.zeros(st_ref.shape, F32)

    dmat = dm_ref[0]
    zeta = zeta_ref[0]
    xi = xi_ref[0]
    cdecay = cd_ref[0]
    for n in range(RET_TM // CHUNK):
        rows = slice(n * CHUNK, (n + 1) * CHUNK)
        qc = q_ref[rows, :]
        kc = k_ref[rows, :]
        vc = v_ref[rows, :]
        state = st_ref[...]
        inner = lax.dot_general(qc, kc, (((1,), (1,)), ((), ())),
                                preferred_element_type=F32) * dmat
        o = jnp.dot(inner.astype(BF16), vc, preferred_element_type=F32)
        o = o + jnp.dot((qc.astype(F32) * xi).astype(BF16), state.astype(BF16),
                        preferred_element_type=F32)
        kz = (kc.astype(F32) * zeta).astype(BF16)
        st_ref[...] = state * cdecay + lax.dot_general(
            kz, vc, (((0,), (0,)), ((), ())), preferred_element_type=F32)
        r = lax.rsqrt(jnp.mean(o * o, axis=-1, keepdims=True) + EPS)
        o_ref[rows, :] = (o * r * _silu(g_ref[rows, :].astype(F32))).astype(BF16)


def retention_tables():
    lg = jnp.log(1.0 - 2.0 ** (-5.0 - jnp.arange(RET_HEADS, dtype=F32)))
    idx = jnp.arange(CHUNK, dtype=F32)
    rel = idx[:, None] - idx[None, :]
    dmat = jnp.where(rel >= 0, jnp.exp(jnp.maximum(rel, 0.0) * lg[:, None, None]), 0.0)
    zeta = jnp.exp((CHUNK - 1 - idx)[None, :] * lg[:, None])[..., None]
    xi = jnp.exp((idx + 1.0)[None, :] * lg[:, None])[..., None]
    cdecay = jnp.exp(CHUNK * lg).reshape(RET_HEADS, 1, 1)
    return dmat, zeta, xi, cdecay


def retention(proj, tabs):
    S = proj.shape[0]
    qb, kb = OFF_QR // RET_QK_DIM, OFF_KR // RET_QK_DIM
    vb, gb = OFF_VR // RET_V_DIM, OFF_GR // RET_V_DIM
    dmat, zeta, xi, cdecay = tabs
    return pl.pallas_call(
        _ret_kernel,
        grid=(RET_HEADS, S // RET_TM),
        in_specs=[pl.BlockSpec((RET_TM, RET_QK_DIM), lambda h, i: (i, qb + h)),
                  pl.BlockSpec((RET_TM, RET_QK_DIM), lambda h, i: (i, kb + h)),
                  pl.BlockSpec((RET_TM, RET_V_DIM), lambda h, i: (i, vb + h)),
                  pl.BlockSpec((RET_TM, RET_V_DIM), lambda h, i: (i, gb + h)),
                  pl.BlockSpec((1, CHUNK, CHUNK), lambda h, i: (h, 0, 0)),
                  pl.BlockSpec((1, CHUNK, 1), lambda h, i: (h, 0, 0)),
                  pl.BlockSpec((1, CHUNK, 1), lambda h, i: (h, 0, 0)),
                  pl.BlockSpec((1, 1, 1), lambda h, i: (h, 0, 0))],
        out_specs=pl.BlockSpec((RET_TM, RET_V_DIM), lambda h, i: (i, h)),
        out_shape=jax.ShapeDtypeStruct((S, RET_WIDTH), BF16),
        scratch_shapes=[pltpu.VMEM((RET_QK_DIM, RET_V_DIM), F32)],
        compiler_params=_cparams(("arbitrary", "arbitrary")),
        name="retention",
    )(proj, proj, proj, proj, dmat, zeta, xi, cdecay)


def _conv_kernel(a_ref, b_ref, ha_ref, hb_ref, g_ref, w_ref, cb_ref, lg_ref, lb_ref,
                 pw_ref, pb_ref, o_ref, u_ref, z_ref, wb_ref, vb_ref):
    i = pl.program_id(0)
    T, H, R = CONV_T, CONV_HALO, SUBLANES
    for j in range(CONV_K):
        wb_ref[j] = jnp.broadcast_to(w_ref[j:j + 1, :], (R, CONV_CH))
    for n, ref in enumerate((cb_ref, lg_ref, lb_ref)):
        vb_ref[n] = jnp.broadcast_to(ref[...], (R, CONV_CH))
    keep = (i > 0).astype(F32)
    u_ref[0, 0:H, :] = (ha_ref[...].astype(F32) * jax.nn.sigmoid(hb_ref[...].astype(F32))
                        * keep)
    u_ref[0, H:H + T, :] = a_ref[...].astype(F32) * jax.nn.sigmoid(b_ref[...].astype(F32))
    u_all = u_ref[0]
    for b in range(1, SUBLANES):
        u_ref[b] = pltpu.roll(u_all, T + H - b, 0)
    first = H - (CONV_K - 1)
    for r in range(T // R):
        acc = vb_ref[0]
        for j in range(CONV_K):
            a, b = divmod(first + j, SUBLANES)
            lo = r * R + a * SUBLANES
            acc = acc + wb_ref[j] * u_ref[b, lo:lo + R, :]
        mu = jnp.mean(acc, axis=-1, keepdims=True)
        d = acc - mu
        var = jnp.mean(d * d, axis=-1, keepdims=True)
        yn = d * lax.rsqrt(var + LN_EPS) * vb_ref[1] + vb_ref[2]
        z_ref[r * R:(r + 1) * R, :] = _silu(yn)
    out = jnp.dot(z_ref[...].astype(BF16), pw_ref[...], preferred_element_type=F32) + pb_ref[...]
    o_ref[...] = (out * _silu(g_ref[...].astype(F32))).astype(BF16)


def conformer_conv(proj, conv_w, conv_b, ln_g, ln_b, pw_w, pw_b):
    S = proj.shape[0]
    T, H = CONV_T, CONV_HALO
    ab, bb, gb = OFF_CA // CONV_CH, OFF_CB // CONV_CH, OFF_GC // CONV_CH
    hpb = T // H
    vec = pl.BlockSpec((1, CONV_CH), lambda i: (0, 0))
    return pl.pallas_call(
        _conv_kernel,
        grid=(S // T,),
        in_specs=[pl.BlockSpec((T, CONV_CH), lambda i: (i, ab)),
                  pl.BlockSpec((T, CONV_CH), lambda i: (i, bb)),
                  pl.BlockSpec((H, CONV_CH), lambda i: (jnp.maximum(i * hpb - 1, 0), ab)),
                  pl.BlockSpec((H, CONV_CH), lambda i: (jnp.maximum(i * hpb - 1, 0), bb)),
                  pl.BlockSpec((T, CONV_CH), lambda i: (i, gb)),
                  pl.BlockSpec((CONV_K, CONV_CH), lambda i: (0, 0)),
                  vec, vec, vec,
                  pl.BlockSpec((CONV_CH, CONV_CH), lambda i: (0, 0)),
                  vec],
        out_specs=pl.BlockSpec((T, CONV_CH), lambda i: (i, 0)),
        out_shape=jax.ShapeDtypeStruct((S, CONV_CH), BF16),
        scratch_shapes=[pltpu.VMEM((SUBLANES, T + H, CONV_CH), F32),
                        pltpu.VMEM((T, CONV_CH), F32),
                        pltpu.VMEM((CONV_K, SUBLANES, CONV_CH), F32),
                        pltpu.VMEM((3, SUBLANES, CONV_CH), F32)],
        compiler_params=_cparams(("arbitrary",)),
        name="conformer_conv",
    )(proj, proj, proj, proj, proj, conv_w, conv_b.reshape(1, -1), ln_g.reshape(1, -1),
      ln_b.reshape(1, -1), pw_w, pw_b.reshape(1, -1))


def _outproj_kernel(ya_ref, yr_ref, yc_ref, w_ref, o_ref):
    a_end = DA_WIDTH
    r_end = DA_WIDTH + RET_WIDTH
    acc = jnp.dot(ya_ref[...], w_ref[0:a_end, :], preferred_element_type=F32)
    acc = acc + jnp.dot(yr_ref[...], w_ref[a_end:r_end, :], preferred_element_type=F32)
    acc = acc + jnp.dot(yc_ref[...], w_ref[r_end:, :], preferred_element_type=F32)
    o_ref[...] = acc


def out_projection(ya, yr, yc, w):
    S = ya.shape[0]
    K, N = w.shape
    return pl.pallas_call(
        _outproj_kernel,
        grid=(S // OUT_TM, N // OUT_TN),
        in_specs=[pl.BlockSpec((OUT_TM, DA_WIDTH), lambda i, j: (i, 0)),
                  pl.BlockSpec((OUT_TM, RET_WIDTH), lambda i, j: (i, 0)),
                  pl.BlockSpec((OUT_TM, CONV_CH), lambda i, j: (i, 0)),
                  pl.BlockSpec((K, OUT_TN), lambda i, j: (0, j))],
        out_specs=pl.BlockSpec((OUT_TM, OUT_TN), lambda i, j: (i, j)),
        out_shape=jax.ShapeDtypeStruct((S, N), F32),
        compiler_params=_cparams(("arbitrary", "arbitrary")),
        name="out_projection",
    )(ya, yr, yc, w)


def kernel(x, pre_norm_g, w_in, lam_q1, lam_k1, lam_q2, lam_k2, diff_subln_g, conv_w, conv_b,
           conv_ln_g, conv_ln_b, conv_pw_w, conv_pw_b, w_out, post_norm_g):
    B, S, D = x.shape
    assert (B, S, D) == (1, SEQ, D_MODEL)
    xs = x.reshape(S, D)
    rtabs = rope_tables(S)
    ret_tabs = retention_tables()
    h = prenorm(xs, pre_norm_g[0])
    for i in range(DEPTH):
        lam_init = 0.8 - 0.6 * math.exp(-0.3 * i)
        proj, v_t = in_projection(h, w_in[i].astype(BF16), rtabs)
        ya = diff_attention(proj, v_t, lam_q1[i], lam_k1[i], lam_q2[i], lam_k2[i],
                            diff_subln_g[i], lam_init)
        yr = retention(proj, ret_tabs)
        yc = conformer_conv(proj, conv_w[i], conv_b[i], conv_ln_g[i], conv_ln_b[i],
                            conv_pw_w[i].astype(BF16), conv_pw_b[i])
        mix = out_projection(ya, yr, yc, w_out[i].astype(BF16))
        g_next = pre_norm_g[(i + 1) % DEPTH]
        xs, h = post_norm_residual(mix, xs, post_norm_g[i], g_next)
    return xs.reshape(B, S, D)
```

```python
import functools
import math

import jax
import jax.numpy as jnp
from jax import lax
from jax.experimental import pallas as pl
from jax.experimental.pallas import tpu as pltpu

D_MODEL = 4096
SEQ = 8192
DEPTH = 4
DA_WIDTH = 2048
DA_HEAD = 128
DA_HEADS = 8
DA_VDIM = 2 * DA_HEAD
ROPE_DIM = 32
ROPE_THETA = 500000.0
RET_WIDTH = 1024
RET_HEADS = 4
RET_V_DIM = 256
RET_QK_DIM = 128
RET_QK_WIDTH = 512
RET_THETA = 10000.0
CONV_CH = 1024
CONV_K = 31
CHUNK = 128
IN_COLS = 14336
EPS = 1e-6
LN_EPS = 1e-5

OFF_QA = 0
OFF_KA = 2048
OFF_VA = 4096
OFF_GA = 6144
OFF_QR = 8192
OFF_KR = 8704
OFF_VR = 9216
OFF_GR = 10240
OFF_CA = 11264
OFF_CB = 12288
OFF_GC = 13312

LANES = 128
SUBLANES = 8

PRE_TM = 256
INP_TM = 2048
INP_TN = 512
ATT_T = 512
ATT_TQ = 1024
RET_TM = 1024
CONV_T = 256
CONV_HALO = 32
OUT_TM = 1024
OUT_TN = 512
VMEM_LIMIT = 56 * 1024 * 1024
LOG2E = math.log2(math.e)

F32 = jnp.float32
BF16 = jnp.bfloat16


def _cparams(sem):
    return pltpu.CompilerParams(dimension_semantics=sem, vmem_limit_bytes=VMEM_LIMIT)


def _silu(x):
    return x * jax.nn.sigmoid(x)


def _prenorm_kernel(x_ref, g_ref, h_ref):
    x = x_ref[...]
    r = lax.rsqrt(jnp.mean(x * x, axis=-1, keepdims=True) + EPS)
    h_ref[...] = (x * r * g_ref[...]).astype(BF16)


def prenorm(x, g):
    S, D = x.shape
    return pl.pallas_call(
        _prenorm_kernel,
        grid=(S // PRE_TM,),
        in_specs=[pl.BlockSpec((PRE_TM, D), lambda i: (i, 0)),
                  pl.BlockSpec((1, D), lambda i: (0, 0))],
        out_specs=pl.BlockSpec((PRE_TM, D), lambda i: (i, 0)),
        out_shape=jax.ShapeDtypeStruct((S, D), BF16),
        compiler_params=_cparams(("arbitrary",)),
        name="prenorm",
    )(x, g.reshape(1, D))


def _post_kernel(mix_ref, x_ref, gp_ref, gn_ref, xo_ref, h_ref):
    mix = mix_ref[...]
    r = lax.rsqrt(jnp.mean(mix * mix, axis=-1, keepdims=True) + EPS)
    xn = x_ref[...] + mix * r * gp_ref[...]
    xo_ref[...] = xn
    r2 = lax.rsqrt(jnp.mean(xn * xn, axis=-1, keepdims=True) + EPS)
    h_ref[...] = (xn * r2 * gn_ref[...]).astype(BF16)


def post_norm_residual(mix, x, g_post, g_next):
    S, D = x.shape
    row = pl.BlockSpec((PRE_TM, D), lambda i: (i, 0))
    vec = pl.BlockSpec((1, D), lambda i: (0, 0))
    return pl.pallas_call(
        _post_kernel,
        grid=(S // PRE_TM,),
        in_specs=[row, row, vec, vec],
        out_specs=[row, row],
        out_shape=[jax.ShapeDtypeStruct((S, D), F32), jax.ShapeDtypeStruct((S, D), BF16)],
        compiler_params=_cparams(("arbitrary",)),
        name="post_norm_residual",
    )(mix, x, g_post.reshape(1, D), g_next.reshape(1, D))


def _inproj_kernel(h_ref, w_ref, ca_ref, sa_ref, sb_ref, cr_ref, sr_ref, o_ref, vt_ref):
    col0 = pl.program_id(1) * INP_TN
    acc = jnp.dot(h_ref[...], w_ref[...].astype(BF16), preferred_element_type=F32)
    o_ref[...] = acc.astype(BF16)
    n_groups = INP_TN // LANES

    @pl.when(jnp.logical_and(col0 >= OFF_VA, col0 < OFF_GA))
    def _():
        acc_t = acc.T
        for b in range(INP_TM // ATT_T):
            vt_ref[b] = acc_t[:, b * ATT_T:(b + 1) * ATT_T].astype(BF16)

    @pl.when(col0 < OFF_VA)
    def _():
        sc = jnp.where(col0 < OFF_KA, DA_HEAD ** -0.5 * LOG2E, 1.0).astype(F32)
        c, sa, sb = ca_ref[...], sa_ref[...], sb_ref[...]
        for g in range(n_groups):
            xg = acc[:, g * LANES:(g + 1) * LANES]
            y = (xg * c + pltpu.roll(xg, LANES - ROPE_DIM // 2, 1) * sa
                 + pltpu.roll(xg, ROPE_DIM // 2, 1) * sb)
            o_ref[:, g * LANES:(g + 1) * LANES] = (y * sc).astype(BF16)

    @pl.when(jnp.logical_and(col0 >= OFF_QR, col0 < OFF_VR))
    def _():
        c, s = cr_ref[...], sr_ref[...]
        for g in range(n_groups):
            xg = acc[:, g * LANES:(g + 1) * LANES]
            sc = jnp.where(col0 + g * LANES >= OFF_KR, RET_QK_DIM ** -0.5, 1.0).astype(F32)
            y = (xg * c + pltpu.roll(xg, RET_QK_DIM // 2, 1) * s) * sc
            o_ref[:, g * LANES:(g + 1) * LANES] = y.astype(BF16)


def in_projection(h, w_all, layer, tabs):
    S, D = h.shape
    N = w_all.shape[2]
    for off in (OFF_KA, OFF_VA, OFF_GA, OFF_QR, OFF_VR):
        assert off % INP_TN == 0
    tab = pl.BlockSpec((INP_TM, LANES), lambda i, j: (i, 0), pipeline_mode=pl.Buffered(1))
    va_first = OFF_VA // INP_TN
    va_last = OFF_GA // INP_TN - 1
    kb_per_tile = INP_TM // ATT_T
    vt_spec = pl.BlockSpec((kb_per_tile, INP_TN, ATT_T),
                           lambda i, j: (i, jnp.clip(j, va_first, va_last) - va_first, 0))
    return pl.pallas_call(
        _inproj_kernel,
        grid=(S // INP_TM, N // INP_TN),
        in_specs=[pl.BlockSpec((INP_TM, D), lambda i, j: (i, 0), pipeline_mode=pl.Buffered(1)),
                  pl.BlockSpec((None, D, INP_TN), lambda i, j: (layer, 0, j)),
                  tab, tab, tab, tab, tab],
        out_specs=[pl.BlockSpec((INP_TM, INP_TN), lambda i, j: (i, j)), vt_spec],
        out_shape=[jax.ShapeDtypeStruct((S, N), BF16),
                   jax.ShapeDtypeStruct((S // ATT_T, DA_WIDTH, ATT_T), BF16)],
        compiler_params=_cparams(("arbitrary", "arbitrary")),
        name="in_projection",
    )(h, w_all, *tabs)


def rope_tables(S):
    pos = jnp.arange(S, dtype=F32)
    lane = jnp.arange(LANES)
    half = ROPE_DIM // 2
    inv = 1.0 / (ROPE_THETA ** (jnp.arange(0, ROPE_DIM, 2, dtype=F32) / ROPE_DIM))
    ang = pos[:, None] * inv[None, :]
    cos, sin = jnp.cos(ang), jnp.sin(ang)
    ones = jnp.ones((S, LANES - ROPE_DIM), F32)
    zeros = jnp.zeros((S, LANES - half), F32)
    ca = jnp.concatenate([cos, cos, ones], axis=1)
    sa = jnp.concatenate([-sin, zeros], axis=1)
    sb = jnp.concatenate([jnp.zeros((S, half), F32), sin,
                          jnp.zeros((S, LANES - ROPE_DIM), F32)], axis=1)
    inv_r = 1.0 / (RET_THETA ** (jnp.arange(0, RET_QK_DIM, 2, dtype=F32) / RET_QK_DIM))
    ang_r = pos[:, None] * inv_r[None, :]
    cos_r, sin_r = jnp.cos(ang_r), jnp.sin(ang_r)
    cr = jnp.concatenate([cos_r, cos_r], axis=1)
    sr = jnp.concatenate([-sin_r, sin_r], axis=1)
    del lane
    return ca, sa, sb, cr, sr


def _attn_kernel(lam_init, q_ref, k_ref, vt_ref, g_ref, lq1_ref, lk1_ref, lq2_ref, lk2_ref,
                 sg_ref, o_ref, acc_ref, sa_ref, sb_ref):
    TQ, TK = ATT_TQ, ATT_T
    NT = (((1,), (1,)), ((), ()))
    qi = pl.program_id(1)
    acc_ref[...] = jnp.zeros(acc_ref.shape, F32)

    def scores(kb, s_ref, cols, diag):
        start = pl.multiple_of(kb * TK, TK)
        n = cols.stop - cols.start
        mb = []
        for c in range(2):
            k = k_ref[pl.ds(start, TK), c * DA_HEAD:(c + 1) * DA_HEAD]
            q = q_ref[cols, c * DA_HEAD:(c + 1) * DA_HEAD]
            s = lax.dot_general(k, q, NT, preferred_element_type=F32)
            if diag:
                key = lax.broadcasted_iota(jnp.int32, (TK, n), 0)
                qry = lax.broadcasted_iota(jnp.int32, (TK, n), 1)
                s = jnp.where(key <= qry, s, -jnp.inf)
            s_ref[c, :, cols] = s
            mb.append(jnp.max(s, axis=0, keepdims=True))
        return tuple(mb)

    def softmax_pv(kb, s_ref, mb, stats, cols, diag_cols=0):
        vt = vt_ref[kb]
        n = cols.stop - cols.start
        out = []
        for c in range(2):
            m_prev, l_prev = stats[2 * c][:, cols], stats[2 * c + 1][:, cols]
            s = s_ref[c, :, cols]
            m_blk = mb[c] if mb is not None else None
            if diag_cols:
                key = lax.broadcasted_iota(jnp.int32, (TK, n), 0)
                qry = lax.broadcasted_iota(jnp.int32, (TK, n), 1)
                s = jnp.where(jnp.logical_or(key <= qry, qry >= diag_cols), s, -jnp.inf)
                m_blk = jnp.max(s, axis=0, keepdims=True)
            m_new = jnp.maximum(m_prev, m_blk)
            alpha = jnp.exp2(m_prev - m_new)
            p = jnp.exp2(s - m_new)
            l_new = alpha * l_prev + jnp.sum(p, axis=0, keepdims=True)
            acc_ref[c, :, cols] = alpha * acc_ref[c, :, cols] + jnp.dot(
                vt, p.astype(BF16), preferred_element_type=F32)
            if n == TQ:
                out += [m_new, l_new]
            else:
                out += [jnp.concatenate([stats[2 * c][:, :cols.start], m_new], axis=1),
                        jnp.concatenate([stats[2 * c + 1][:, :cols.start], l_new], axis=1)]
        return tuple(out)

    full = slice(0, TQ)
    late = slice(TQ - TK, TQ)
    neg = jnp.full((1, TQ), -jnp.inf, F32)
    zero = jnp.zeros((1, TQ), F32)

    def body(j, carry):
        stats, mb_a = carry[:4], carry[4:]
        mb_b = scores(2 * j + 1, sb_ref, full, False)
        stats = softmax_pv(2 * j, sa_ref, mb_a, stats, full)
        mb_a = scores(2 * j + 2, sa_ref, full, False)
        stats = softmax_pv(2 * j + 1, sb_ref, mb_b, stats, full)
        return stats + mb_a

    mb_a = scores(0, sa_ref, full, False)
    carry = lax.fori_loop(0, qi, body, (neg, zero, neg, zero) + mb_a)
    stats = carry[:4]
    mb_b = scores(2 * qi + 1, sb_ref, late, True)
    stats = softmax_pv(2 * qi, sa_ref, None, stats, full, diag_cols=TK)
    _, l0, _, l1 = softmax_pv(2 * qi + 1, sb_ref, mb_b, stats, late)

    lam = (jnp.exp(jnp.sum(lq1_ref[...] * lk1_ref[...], axis=-1, keepdims=True))
           - jnp.exp(jnp.sum(lq2_ref[...] * lk2_ref[...], axis=-1, keepdims=True))
           + lam_init)
    o_t = acc_ref[0] * (1.0 / l0) - lam * (acc_ref[1] * (1.0 / l1))
    o = o_t.T
    r = lax.rsqrt(jnp.mean(o * o, axis=-1, keepdims=True) + EPS)
    y = o * r * sg_ref[...] * (1.0 - lam_init)
    o_ref[...] = (y * _silu(g_ref[...].astype(F32))).astype(BF16)


def diff_attention(proj, v_t, lq1, lk1, lq2, lk2, subln_g, lam_init):
    S = proj.shape[0]
    TQ, TK = ATT_TQ, ATT_T
    assert TQ == 2 * TK
    qb, kb, gb = OFF_QA // DA_VDIM, OFF_KA // DA_VDIM, OFF_GA // DA_VDIM
    vec = pl.BlockSpec((1, DA_HEAD), lambda h, i: (0, 0))
    return pl.pallas_call(
        functools.partial(_attn_kernel, lam_init),
        grid=(DA_HEADS, S // TQ),
        in_specs=[pl.BlockSpec((TQ, DA_VDIM), lambda h, i: (i, qb + h)),
                  pl.BlockSpec((S, DA_VDIM), lambda h, i: (0, kb + h)),
                  pl.BlockSpec((S // TK, DA_VDIM, TK), lambda h, i: (0, h, 0)),
                  pl.BlockSpec((TQ, DA_VDIM), lambda h, i: (i, gb + h)),
                  vec, vec, vec, vec,
                  pl.BlockSpec((1, DA_VDIM), lambda h, i: (0, 0))],
        out_specs=pl.BlockSpec((TQ, DA_VDIM), lambda h, i: (i, h)),
        out_shape=jax.ShapeDtypeStruct((S, DA_WIDTH), BF16),
        scratch_shapes=[pltpu.VMEM((2, DA_VDIM, TQ), F32),
                        pltpu.VMEM((2, TK, TQ), F32), pltpu.VMEM((2, TK, TQ), F32)],
        compiler_params=_cparams(("arbitrary", "arbitrary")),
        name="diff_attention",
    )(proj, proj, v_t, proj, lq1.reshape(1, -1), lk1.reshape(1, -1), lq2.reshape(1, -1),
      lk2.reshape(1, -1), subln_g.reshape(1, -1))


def _ret_kernel(q_ref, k_ref, v_ref, g_ref, dm_ref, zeta_ref, xi_ref, cd_ref, o_ref, st_ref):
    @pl.when(pl.program_id(1) == 0)
    def _():
        st_ref[...] = jnp.zeros(st_ref.shape, F32)

    dmat = dm_ref[0]
    zeta = zeta_ref[0]
    xi = xi_ref[0]
    cdecay = cd_ref[0]
    for n in range(RET_TM // CHUNK):
        rows = slice(n * CHUNK, (n + 1) * CHUNK)
        qc = q_ref[rows, :]
        kc = k_ref[rows, :]
        vc = v_ref[rows, :]
        state = st_ref[...]
        inner = lax.dot_general(qc, kc, (((1,), (1,)), ((), ())),
                                preferred_element_type=F32) * dmat
        o = jnp.dot(inner.astype(BF16), vc, preferred_element_type=F32)
        o = o + jnp.dot((qc.astype(F32) * xi).astype(BF16), state.astype(BF16),
                        preferred_element_type=F32)
        kz = (kc.astype(F32) * zeta).astype(BF16)
        st_ref[...] = state * cdecay + lax.dot_general(
            kz, vc, (((0,), (0,)), ((), ())), preferred_element_type=F32)
        r = lax.rsqrt(jnp.mean(o * o, axis=-1, keepdims=True) + EPS)
        o_ref[rows, :] = (o * r * _silu(g_ref[rows, :].astype(F32))).astype(BF16)


def retention_tables():
    lg = jnp.log(1.0 - 2.0 ** (-5.0 - jnp.arange(RET_HEADS, dtype=F32)))
    idx = jnp.arange(CHUNK, dtype=F32)
    rel = idx[:, None] - idx[None, :]
    dmat = jnp.where(rel >= 0, jnp.exp(jnp.maximum(rel, 0.0) * lg[:, None, None]), 0.0)
    zeta = jnp.exp((CHUNK - 1 - idx)[None, :] * lg[:, None])[..., None]
    xi = jnp.exp((idx + 1.0)[None, :] * lg[:, None])[..., None]
    cdecay = jnp.exp(CHUNK * lg).reshape(RET_HEADS, 1, 1)
    return dmat, zeta, xi, cdecay


def retention(proj, tabs):
    S = proj.shape[0]
    qb, kb = OFF_QR // RET_QK_DIM, OFF_KR // RET_QK_DIM
    vb, gb = OFF_VR // RET_V_DIM, OFF_GR // RET_V_DIM
    dmat, zeta, xi, cdecay = tabs
    return pl.pallas_call(
        _ret_kernel,
        grid=(RET_HEADS, S // RET_TM),
        in_specs=[pl.BlockSpec((RET_TM, RET_QK_DIM), lambda h, i: (i, qb + h)),
                  pl.BlockSpec((RET_TM, RET_QK_DIM), lambda h, i: (i, kb + h)),
                  pl.BlockSpec((RET_TM, RET_V_DIM), lambda h, i: (i, vb + h)),
                  pl.BlockSpec((RET_TM, RET_V_DIM), lambda h, i: (i, gb + h)),
                  pl.BlockSpec((1, CHUNK, CHUNK), lambda h, i: (h, 0, 0)),
                  pl.BlockSpec((1, CHUNK, 1), lambda h, i: (h, 0, 0)),
                  pl.BlockSpec((1, CHUNK, 1), lambda h, i: (h, 0, 0)),
                  pl.BlockSpec((1, 1, 1), lambda h, i: (h, 0, 0))],
        out_specs=pl.BlockSpec((RET_TM, RET_V_DIM), lambda h, i: (i, h)),
        out_shape=jax.ShapeDtypeStruct((S, RET_WIDTH), BF16),
        scratch_shapes=[pltpu.VMEM((RET_QK_DIM, RET_V_DIM), F32)],
        compiler_params=_cparams(("arbitrary", "arbitrary")),
        name="retention",
    )(proj, proj, proj, proj, dmat, zeta, xi, cdecay)


def _conv_kernel(a_ref, b_ref, ha_ref, hb_ref, g_ref, w_ref, cb_ref, lg_ref, lb_ref,
                 pw_ref, pb_ref, o_ref, u_ref, z_ref, wb_ref, vb_ref):
    i = pl.program_id(0)
    T, H, R = CONV_T, CONV_HALO, SUBLANES
    for j in range(CONV_K):
        wb_ref[j] = jnp.broadcast_to(w_ref[j:j + 1, :], (R, CONV_CH))
    for n, ref in enumerate((cb_ref, lg_ref, lb_ref)):
        vb_ref[n] = jnp.broadcast_to(ref[...], (R, CONV_CH))
    keep = (i > 0).astype(F32)
    u_ref[0, 0:H, :] = (ha_ref[...].astype(F32) * jax.nn.sigmoid(hb_ref[...].astype(F32))
                        * keep)
    u_ref[0, H:H + T, :] = a_ref[...].astype(F32) * jax.nn.sigmoid(b_ref[...].astype(F32))
    u_all = u_ref[0]
    for b in range(1, SUBLANES):
        u_ref[b] = pltpu.roll(u_all, T + H - b, 0)
    first = H - (CONV_K - 1)
    for r in range(T // R):
        acc = vb_ref[0]
        for j in range(CONV_K):
            a, b = divmod(first + j, SUBLANES)
            lo = r * R + a * SUBLANES
            acc = acc + wb_ref[j] * u_ref[b, lo:lo + R, :]
        mu = jnp.mean(acc, axis=-1, keepdims=True)
        d = acc - mu
        var = jnp.mean(d * d, axis=-1, keepdims=True)
        yn = d * lax.rsqrt(var + LN_EPS) * vb_ref[1] + vb_ref[2]
        z_ref[r * R:(r + 1) * R, :] = _silu(yn)
    out = jnp.dot(z_ref[...].astype(BF16), pw_ref[...], preferred_element_type=F32) + pb_ref[...]
    o_ref[...] = (out * _silu(g_ref[...].astype(F32))).astype(BF16)


def conformer_conv(proj, conv_w, conv_b, ln_g, ln_b, pw_w, pw_b):
    S = proj.shape[0]
    T, H = CONV_T, CONV_HALO
    ab, bb, gb = OFF_CA // CONV_CH, OFF_CB // CONV_CH, OFF_GC // CONV_CH
    hpb = T // H
    vec = pl.BlockSpec((1, CONV_CH), lambda i: (0, 0))
    return pl.pallas_call(
        _conv_kernel,
        grid=(S // T,),
        in_specs=[pl.BlockSpec((T, CONV_CH), lambda i: (i, ab)),
                  pl.BlockSpec((T, CONV_CH), lambda i: (i, bb)),
                  pl.BlockSpec((H, CONV_CH), lambda i: (jnp.maximum(i * hpb - 1, 0), ab)),
                  pl.BlockSpec((H, CONV_CH), lambda i: (jnp.maximum(i * hpb - 1, 0), bb)),
                  pl.BlockSpec((T, CONV_CH), lambda i: (i, gb)),
                  pl.BlockSpec((CONV_K, CONV_CH), lambda i: (0, 0)),
                  vec, vec, vec,
                  pl.BlockSpec((CONV_CH, CONV_CH), lambda i: (0, 0)),
                  vec],
        out_specs=pl.BlockSpec((T, CONV_CH), lambda i: (i, 0)),
        out_shape=jax.ShapeDtypeStruct((S, CONV_CH), BF16),
        scratch_shapes=[pltpu.VMEM((SUBLANES, T + H, CONV_CH), F32),
                        pltpu.VMEM((T, CONV_CH), F32),
                        pltpu.VMEM((CONV_K, SUBLANES, CONV_CH), F32),
                        pltpu.VMEM((3, SUBLANES, CONV_CH), F32)],
        compiler_params=_cparams(("arbitrary",)),
        name="conformer_conv",
    )(proj, proj, proj, proj, proj, conv_w, conv_b.reshape(1, -1), ln_g.reshape(1, -1),
      ln_b.reshape(1, -1), pw_w, pw_b.reshape(1, -1))


def _outproj_kernel(ya_ref, yr_ref, yc_ref, w_ref, o_ref):
    a_end = DA_WIDTH
    r_end = DA_WIDTH + RET_WIDTH
    acc = jnp.dot(ya_ref[...], w_ref[0:a_end, :].astype(BF16), preferred_element_type=F32)
    acc = acc + jnp.dot(yr_ref[...], w_ref[a_end:r_end, :].astype(BF16),
                        preferred_element_type=F32)
    acc = acc + jnp.dot(yc_ref[...], w_ref[r_end:, :].astype(BF16),
                        preferred_element_type=F32)
    o_ref[...] = acc


def out_projection(ya, yr, yc, w_all, layer):
    S = ya.shape[0]
    _, K, N = w_all.shape
    once = pl.Buffered(1)
    return pl.pallas_call(
        _outproj_kernel,
        grid=(S // OUT_TM, N // OUT_TN),
        in_specs=[pl.BlockSpec((OUT_TM, DA_WIDTH), lambda i, j: (i, 0), pipeline_mode=once),
                  pl.BlockSpec((OUT_TM, RET_WIDTH), lambda i, j: (i, 0), pipeline_mode=once),
                  pl.BlockSpec((OUT_TM, CONV_CH), lambda i, j: (i, 0), pipeline_mode=once),
                  pl.BlockSpec((None, K, OUT_TN), lambda i, j: (layer, 0, j))],
        out_specs=pl.BlockSpec((OUT_TM, OUT_TN), lambda i, j: (i, j)),
        out_shape=jax.ShapeDtypeStruct((S, N), F32),
        compiler_params=_cparams(("arbitrary", "arbitrary")),
        name="out_projection",
    )(ya, yr, yc, w_all)


def kernel(x, pre_norm_g, w_in, lam_q1, lam_k1, lam_q2, lam_k2, diff_subln_g, conv_w, conv_b,
           conv_ln_g, conv_ln_b, conv_pw_w, conv_pw_b, w_out, post_norm_g):
    B, S, D = x.shape
    assert (B, S, D) == (1, SEQ, D_MODEL)
    xs = x.reshape(S, D)
    rtabs = rope_tables(S)
    ret_tabs = retention_tables()
    h = prenorm(xs, pre_norm_g[0])
    for i in range(DEPTH):
        lam_init = 0.8 - 0.6 * math.exp(-0.3 * i)
        proj, v_t = in_projection(h, w_in, i, rtabs)
        ya = diff_attention(proj, v_t, lam_q1[i], lam_k1[i], lam_q2[i], lam_k2[i],
                            diff_subln_g[i], lam_init)
        yr = retention(proj, ret_tabs)
        yc = conformer_conv(proj, conv_w[i], conv_b[i], conv_ln_g[i], conv_ln_b[i],
                            conv_pw_w[i].astype(BF16), conv_pw_b[i])
        mix = out_projection(ya, yr, yc, w_out, i)
        g_next = pre_norm_g[(i + 1) % DEPTH]
        xs, h = post_norm_residual(mix, xs, post_norm_g[i], g_next)
    return xs.reshape(B, S, D)
```

```python
import functools
import math

import jax
import jax.numpy as jnp
from jax import lax
from jax.experimental import pallas as pl
from jax.experimental.pallas import tpu as pltpu

D_MODEL = 4096
SEQ = 8192
DEPTH = 4
DA_WIDTH = 2048
DA_HEAD = 128
DA_HEADS = 8
DA_VDIM = 2 * DA_HEAD
ROPE_DIM = 32
ROPE_THETA = 500000.0
RET_WIDTH = 1024
RET_HEADS = 4
RET_V_DIM = 256
RET_QK_DIM = 128
RET_QK_WIDTH = 512
RET_THETA = 10000.0
CONV_CH = 1024
CONV_K = 31
CHUNK = 128
IN_COLS = 14336
EPS = 1e-6
LN_EPS = 1e-5

OFF_QA = 0
OFF_KA = 2048
OFF_VA = 4096
OFF_GA = 6144
OFF_QR = 8192
OFF_KR = 8704
OFF_VR = 9216
OFF_GR = 10240
OFF_CA = 11264
OFF_CB = 12288
OFF_GC = 13312

LANES = 128
SUBLANES = 8

PRE_TM = 256
INP_TM = 1024
INP_TN = 512
ATT_T = 512
ATT_TQ = 1024
RET_TM = 1024
CONV_T = 256
CONV_HALO = 32
OUT_TM = 2048
OUT_TN = 512
VMEM_LIMIT = 56 * 1024 * 1024
LOG2E = math.log2(math.e)

F32 = jnp.float32
BF16 = jnp.bfloat16


def _cparams(sem):
    return pltpu.CompilerParams(dimension_semantics=sem, vmem_limit_bytes=VMEM_LIMIT)


def _silu(x):
    return x * jax.nn.sigmoid(x)


def _prenorm_kernel(x_ref, g_ref, h_ref):
    x = x_ref[...]
    r = lax.rsqrt(jnp.mean(x * x, axis=-1, keepdims=True) + EPS)
    h_ref[...] = (x * r * g_ref[...]).astype(BF16)


def prenorm(x, g):
    S, D = x.shape
    return pl.pallas_call(
        _prenorm_kernel,
        grid=(S // PRE_TM,),
        in_specs=[pl.BlockSpec((PRE_TM, D), lambda i: (i, 0)),
                  pl.BlockSpec((1, D), lambda i: (0, 0))],
        out_specs=pl.BlockSpec((PRE_TM, D), lambda i: (i, 0)),
        out_shape=jax.ShapeDtypeStruct((S, D), BF16),
        compiler_params=_cparams(("arbitrary",)),
        name="prenorm",
    )(x, g.reshape(1, D))


def _post_kernel(mix_ref, x_ref, gp_ref, gn_ref, xo_ref, h_ref):
    mix = mix_ref[...]
    r = lax.rsqrt(jnp.mean(mix * mix, axis=-1, keepdims=True) + EPS)
    xn = x_ref[...] + mix * r * gp_ref[...]
    xo_ref[...] = xn
    r2 = lax.rsqrt(jnp.mean(xn * xn, axis=-1, keepdims=True) + EPS)
    h_ref[...] = (xn * r2 * gn_ref[...]).astype(BF16)


def post_norm_residual(mix, x, g_post, g_next):
    S, D = x.shape
    row = pl.BlockSpec((PRE_TM, D), lambda i: (i, 0))
    vec = pl.BlockSpec((1, D), lambda i: (0, 0))
    return pl.pallas_call(
        _post_kernel,
        grid=(S // PRE_TM,),
        in_specs=[row, row, vec, vec],
        out_specs=[row, row],
        out_shape=[jax.ShapeDtypeStruct((S, D), F32), jax.ShapeDtypeStruct((S, D), BF16)],
        compiler_params=_cparams(("arbitrary",)),
        name="post_norm_residual",
    )(mix, x, g_post.reshape(1, D), g_next.reshape(1, D))


EPI_PLAIN, EPI_QA, EPI_KA, EPI_QR, EPI_KR = range(5)


def _epilogue_kind(col0):
    is_qr = jnp.logical_and(col0 >= OFF_QR, col0 < OFF_KR)
    is_kr = jnp.logical_and(col0 >= OFF_KR, col0 < OFF_VR)
    return jnp.where(col0 < OFF_KA, EPI_QA,
                     jnp.where(col0 < OFF_VA, EPI_KA,
                               jnp.where(is_qr, EPI_QR, jnp.where(is_kr, EPI_KR, EPI_PLAIN))))


def _inproj_kernel(h_ref, w_ref, tab_ref, o_ref, vt_ref, acc0_ref, acc1_ref):
    t = pl.program_id(0)
    n_j = IN_COLS // INP_TN
    col0 = (jnp.maximum(t - 1, 0) % n_j) * INP_TN
    is_diff = col0 < OFF_VA
    is_ret = jnp.logical_and(col0 >= OFF_QR, col0 < OFF_VR)
    half = ROPE_DIM // 2
    shift1 = jnp.where(is_diff, LANES - half, jnp.where(is_ret, RET_QK_DIM // 2, 0))
    shift2 = jnp.where(is_diff, half, 0)

    @pl.when(t == 0)
    def _():
        acc1_ref[...] = jnp.zeros(acc1_ref.shape, F32)

    def step(acc_w, acc_r, rotary):
        acc_w[...] = jnp.dot(h_ref[...], w_ref[...].astype(BF16), preferred_element_type=F32)
        if rotary:
            c, s1, s2 = tab_ref[0], tab_ref[1], tab_ref[2]
            for g in range(INP_TN // LANES):
                xg = acc_r[:, g * LANES:(g + 1) * LANES]
                y = xg * c + pltpu.roll(xg, shift1, 1) * s1 + pltpu.roll(xg, shift2, 1) * s2
                o_ref[:, g * LANES:(g + 1) * LANES] = y.astype(BF16)
        else:
            o_ref[...] = acc_r[...].astype(BF16)

            @pl.when(jnp.logical_and(col0 >= OFF_VA, col0 < OFF_GA))
            def _():
                acc_t = acc_r[...].T
                for b in range(INP_TM // ATT_T):
                    vt_ref[b] = acc_t[:, b * ATT_T:(b + 1) * ATT_T].astype(BF16)

    even = t % 2 == 0
    rot = jnp.logical_or(is_diff, is_ret)
    for parity, (acc_w, acc_r) in ((even, (acc0_ref, acc1_ref)),
                                   (jnp.logical_not(even), (acc1_ref, acc0_ref))):
        for cond, rotary in ((rot, True), (jnp.logical_not(rot), False)):
            pl.when(jnp.logical_and(parity, cond))(
                functools.partial(step, acc_w, acc_r, rotary))


def in_projection(h, w_all, layer, tabs):
    S, D = h.shape
    N = w_all.shape[2]
    for off in (OFF_KA, OFF_VA, OFF_GA, OFF_QR, OFF_KR, OFF_VR):
        assert off % INP_TN == 0
    n_i, n_j = S // INP_TM, N // INP_TN
    n_tiles = n_i * n_j
    va_first = OFF_VA // INP_TN
    va_last = OFF_GA // INP_TN - 1

    def cur(t):
        tc = jnp.minimum(t, n_tiles - 1)
        return tc // n_j, tc % n_j

    def prev(t):
        tp = jnp.maximum(t - 1, 0)
        return tp // n_j, tp % n_j

    def vt_index(t):
        i, j = prev(t)
        return i, jnp.clip(j, va_first, va_last) - va_first, 0

    def tab_index(t):
        i, j = prev(t)
        return _epilogue_kind(j * INP_TN), 0, i, 0

    return pl.pallas_call(
        _inproj_kernel,
        grid=(n_tiles + 1,),
        in_specs=[pl.BlockSpec((INP_TM, D), lambda t: (cur(t)[0], 0)),
                  pl.BlockSpec((None, D, INP_TN), lambda t: (layer, 0, cur(t)[1])),
                  pl.BlockSpec((None, 3, INP_TM, LANES), tab_index)],
        out_specs=[pl.BlockSpec((INP_TM, INP_TN), prev),
                   pl.BlockSpec((INP_TM // ATT_T, INP_TN, ATT_T), vt_index)],
        out_shape=[jax.ShapeDtypeStruct((S, N), BF16),
                   jax.ShapeDtypeStruct((S // ATT_T, DA_WIDTH, ATT_T), BF16)],
        scratch_shapes=[pltpu.VMEM((INP_TM, INP_TN), F32), pltpu.VMEM((INP_TM, INP_TN), F32)],
        compiler_params=_cparams(("arbitrary",)),
        name="in_projection",
    )(h, w_all, tabs)


def rope_tables(S):
    pos = jnp.arange(S, dtype=F32)
    half = ROPE_DIM // 2
    inv = 1.0 / (ROPE_THETA ** (jnp.arange(0, ROPE_DIM, 2, dtype=F32) / ROPE_DIM))
    ang = pos[:, None] * inv[None, :]
    cos, sin = jnp.cos(ang), jnp.sin(ang)
    one = jnp.ones((S, LANES), F32)
    zero = jnp.zeros((S, LANES), F32)
    ca = jnp.concatenate([cos, cos, one[:, ROPE_DIM:]], axis=1)
    sa = jnp.concatenate([-sin, zero[:, half:]], axis=1)
    sb = jnp.concatenate([zero[:, :half], sin, zero[:, ROPE_DIM:]], axis=1)
    inv_r = 1.0 / (RET_THETA ** (jnp.arange(0, RET_QK_DIM, 2, dtype=F32) / RET_QK_DIM))
    ang_r = pos[:, None] * inv_r[None, :]
    cos_r, sin_r = jnp.cos(ang_r), jnp.sin(ang_r)
    cr = jnp.concatenate([cos_r, cos_r], axis=1)
    sr = jnp.concatenate([-sin_r, sin_r], axis=1)
    q_sc = DA_HEAD ** -0.5 * LOG2E
    k_sc = RET_QK_DIM ** -0.5
    kinds = {EPI_PLAIN: (one, zero, zero),
             EPI_QA: (ca * q_sc, sa * q_sc, sb * q_sc),
             EPI_KA: (ca, sa, sb),
             EPI_QR: (cr, sr, zero),
             EPI_KR: (cr * k_sc, sr * k_sc, zero)}
    return jnp.stack([jnp.stack(kinds[k]) for k in range(len(kinds))])


def _attn_kernel(lam_init, q_ref, k_ref, vt_ref, g_ref, lq1_ref, lk1_ref, lq2_ref, lk2_ref,
                 sg_ref, o_ref, acc_ref, sa_ref, sb_ref):
    TQ, TK = ATT_TQ, ATT_T
    NT = (((1,), (1,)), ((), ()))
    qi = pl.program_id(1)
    acc_ref[...] = jnp.zeros(acc_ref.shape, F32)

    def scores(kb, s_ref, cols, diag):
        start = pl.multiple_of(kb * TK, TK)
        n = cols.stop - cols.start
        mb = []
        for c in range(2):
            k = k_ref[pl.ds(start, TK), c * DA_HEAD:(c + 1) * DA_HEAD]
            q = q_ref[cols, c * DA_HEAD:(c + 1) * DA_HEAD]
            s = lax.dot_general(k, q, NT, preferred_element_type=F32)
            if diag:
                key = lax.broadcasted_iota(jnp.int32, (TK, n), 0)
                qry = lax.broadcasted_iota(jnp.int32, (TK, n), 1)
                s = jnp.where(key <= qry, s, -jnp.inf)
            s_ref[c, :, cols] = s
            mb.append(jnp.max(s, axis=0, keepdims=True))
        return tuple(mb)

    def softmax_pv(kb, s_ref, mb, stats, cols, diag_cols=0):
        vt = vt_ref[kb]
        n = cols.stop - cols.start
        out = []
        for c in range(2):
            m_prev, l_prev = stats[2 * c][:, cols], stats[2 * c + 1][:, cols]
            s = s_ref[c, :, cols]
            m_blk = mb[c] if mb is not None else None
            if diag_cols:
                key = lax.broadcasted_iota(jnp.int32, (TK, n), 0)
                qry = lax.broadcasted_iota(jnp.int32, (TK, n), 1)
                s = jnp.where(jnp.logical_or(key <= qry, qry >= diag_cols), s, -jnp.inf)
                m_blk = jnp.max(s, axis=0, keepdims=True)
            m_new = jnp.maximum(m_prev, m_blk)
            alpha = jnp.exp2(m_prev - m_new)
            p = jnp.exp2(s - m_new)
            l_new = alpha * l_prev + jnp.sum(p, axis=0, keepdims=True)
            acc_ref[c, :, cols] = alpha * acc_ref[c, :, cols] + jnp.dot(
                vt, p.astype(BF16), preferred_element_type=F32)
            if n == TQ:
                out += [m_new, l_new]
            else:
                out += [jnp.concatenate([stats[2 * c][:, :cols.start], m_new], axis=1),
                        jnp.concatenate([stats[2 * c + 1][:, :cols.start], l_new], axis=1)]
        return tuple(out)

    full = slice(0, TQ)
    late = slice(TQ - TK, TQ)
    neg = jnp.full((1, TQ), -jnp.inf, F32)
    zero = jnp.zeros((1, TQ), F32)

    def body(j, carry):
        stats, mb_a = carry[:4], carry[4:]
        mb_b = scores(2 * j + 1, sb_ref, full, False)
        stats = softmax_pv(2 * j, sa_ref, mb_a, stats, full)
        mb_a = scores(2 * j + 2, sa_ref, full, False)
        stats = softmax_pv(2 * j + 1, sb_ref, mb_b, stats, full)
        return stats + mb_a

    mb_a = scores(0, sa_ref, full, False)
    carry = lax.fori_loop(0, qi, body, (neg, zero, neg, zero) + mb_a)
    stats = carry[:4]
    mb_b = scores(2 * qi + 1, sb_ref, late, True)
    stats = softmax_pv(2 * qi, sa_ref, None, stats, full, diag_cols=TK)
    _, l0, _, l1 = softmax_pv(2 * qi + 1, sb_ref, mb_b, stats, late)

    lam = (jnp.exp(jnp.sum(lq1_ref[...] * lk1_ref[...], axis=-1, keepdims=True))
           - jnp.exp(jnp.sum(lq2_ref[...] * lk2_ref[...], axis=-1, keepdims=True))
           + lam_init)
    o_t = acc_ref[0] * (1.0 / l0) - lam * (acc_ref[1] * (1.0 / l1))
    o = o_t.T
    r = lax.rsqrt(jnp.mean(o * o, axis=-1, keepdims=True) + EPS)
    y = o * r * sg_ref[...] * (1.0 - lam_init)
    o_ref[...] = (y * _silu(g_ref[...].astype(F32))).astype(BF16)


def diff_attention(proj, v_t, lq1, lk1, lq2, lk2, subln_g, lam_init):
    S = proj.shape[0]
    TQ, TK = ATT_TQ, ATT_T
    assert TQ == 2 * TK
    qb, kb, gb = OFF_QA // DA_VDIM, OFF_KA // DA_VDIM, OFF_GA // DA_VDIM
    vec = pl.BlockSpec((1, DA_HEAD), lambda h, i: (0, 0))
    return pl.pallas_call(
        functools.partial(_attn_kernel, lam_init),
        grid=(DA_HEADS, S // TQ),
        in_specs=[pl.BlockSpec((TQ, DA_VDIM), lambda h, i: (i, qb + h)),
                  pl.BlockSpec((S, DA_VDIM), lambda h, i: (0, kb + h)),
                  pl.BlockSpec((S // TK, DA_VDIM, TK), lambda h, i: (0, h, 0)),
                  pl.BlockSpec((TQ, DA_VDIM), lambda h, i: (i, gb + h)),
                  vec, vec, vec, vec,
                  pl.BlockSpec((1, DA_VDIM), lambda h, i: (0, 0))],
        out_specs=pl.BlockSpec((TQ, DA_VDIM), lambda h, i: (i, h)),
        out_shape=jax.ShapeDtypeStruct((S, DA_WIDTH), BF16),
        scratch_shapes=[pltpu.VMEM((2, DA_VDIM, TQ), F32),
                        pltpu.VMEM((2, TK, TQ), F32), pltpu.VMEM((2, TK, TQ), F32)],
        compiler_params=_cparams(("arbitrary", "arbitrary")),
        name="diff_attention",
    )(proj, proj, v_t, proj, lq1.reshape(1, -1), lk1.reshape(1, -1), lq2.reshape(1, -1),
      lk2.reshape(1, -1), subln_g.reshape(1, -1))


def _ret_kernel(q_ref, k_ref, v_ref, g_ref, dm_ref, zeta_ref, xi_ref, cd_ref, o_ref, st_ref):
    @pl.when(pl.program_id(1) == 0)
    def _():
        st_ref[...] = jnp.zeros(st_ref.shape, F32)

    dmat = dm_ref[0]
    zeta = zeta_ref[0]
    xi = xi_ref[0]
    cdecay = cd_ref[0]
    for n in range(RET_TM // CHUNK):
        rows = slice(n * CHUNK, (n + 1) * CHUNK)
        qc = q_ref[rows, :]
        kc = k_ref[rows, :]
        vc = v_ref[rows, :]
        state = st_ref[...]
        inner = lax.dot_general(qc, kc, (((1,), (1,)), ((), ())),
                                preferred_element_type=F32) * dmat
        o = jnp.dot(inner.astype(BF16), vc, preferred_element_type=F32)
        o = o + jnp.dot((qc.astype(F32) * xi).astype(BF16), state.astype(BF16),
                        preferred_element_type=F32)
        kz = (kc.astype(F32) * zeta).astype(BF16)
        st_ref[...] = state * cdecay + lax.dot_general(
            kz, vc, (((0,), (0,)), ((), ())), preferred_element_type=F32)
        r = lax.rsqrt(jnp.mean(o * o, axis=-1, keepdims=True) + EPS)
        o_ref[rows, :] = (o * r * _silu(g_ref[rows, :].astype(F32))).astype(BF16)


def retention_tables():
    lg = jnp.log(1.0 - 2.0 ** (-5.0 - jnp.arange(RET_HEADS, dtype=F32)))
    idx = jnp.arange(CHUNK, dtype=F32)
    rel = idx[:, None] - idx[None, :]
    dmat = jnp.where(rel >= 0, jnp.exp(jnp.maximum(rel, 0.0) * lg[:, None, None]), 0.0)
    zeta = jnp.exp((CHUNK - 1 - idx)[None, :] * lg[:, None])[..., None]
    xi = jnp.exp((idx + 1.0)[None, :] * lg[:, None])[..., None]
    cdecay = jnp.exp(CHUNK * lg).reshape(RET_HEADS, 1, 1)
    return dmat, zeta, xi, cdecay


def retention(proj, tabs):
    S = proj.shape[0]
    qb, kb = OFF_QR // RET_QK_DIM, OFF_KR // RET_QK_DIM
    vb, gb = OFF_VR // RET_V_DIM, OFF_GR // RET_V_DIM
    dmat, zeta, xi, cdecay = tabs
    return pl.pallas_call(
        _ret_kernel,
        grid=(RET_HEADS, S // RET_TM),
        in_specs=[pl.BlockSpec((RET_TM, RET_QK_DIM), lambda h, i: (i, qb + h)),
                  pl.BlockSpec((RET_TM, RET_QK_DIM), lambda h, i: (i, kb + h)),
                  pl.BlockSpec((RET_TM, RET_V_DIM), lambda h, i: (i, vb + h)),
                  pl.BlockSpec((RET_TM, RET_V_DIM), lambda h, i: (i, gb + h)),
                  pl.BlockSpec((1, CHUNK, CHUNK), lambda h, i: (h, 0, 0)),
                  pl.BlockSpec((1, CHUNK, 1), lambda h, i: (h, 0, 0)),
                  pl.BlockSpec((1, CHUNK, 1), lambda h, i: (h, 0, 0)),
                  pl.BlockSpec((1, 1, 1), lambda h, i: (h, 0, 0))],
        out_specs=pl.BlockSpec((RET_TM, RET_V_DIM), lambda h, i: (i, h)),
        out_shape=jax.ShapeDtypeStruct((S, RET_WIDTH), BF16),
        scratch_shapes=[pltpu.VMEM((RET_QK_DIM, RET_V_DIM), F32)],
        compiler_params=_cparams(("arbitrary", "arbitrary")),
        name="retention",
    )(proj, proj, proj, proj, dmat, zeta, xi, cdecay)


def _conv_kernel(a_ref, b_ref, ha_ref, hb_ref, g_ref, w_ref, cb_ref, lg_ref, lb_ref,
                 pw_ref, pb_ref, o_ref, u_ref, z_ref, wb_ref, vb_ref):
    i = pl.program_id(0)
    T, H, R = CONV_T, CONV_HALO, SUBLANES
    for j in range(CONV_K):
        wb_ref[j] = jnp.broadcast_to(w_ref[j:j + 1, :], (R, CONV_CH))
    for n, ref in enumerate((cb_ref, lg_ref, lb_ref)):
        vb_ref[n] = jnp.broadcast_to(ref[...], (R, CONV_CH))
    keep = (i > 0).astype(F32)
    u_ref[0, 0:H, :] = (ha_ref[...].astype(F32) * jax.nn.sigmoid(hb_ref[...].astype(F32))
                        * keep)
    u_ref[0, H:H + T, :] = a_ref[...].astype(F32) * jax.nn.sigmoid(b_ref[...].astype(F32))
    u_all = u_ref[0]
    for b in range(1, SUBLANES):
        u_ref[b] = pltpu.roll(u_all, T + H - b, 0)
    first = H - (CONV_K - 1)
    for r in range(T // R):
        acc = vb_ref[0]
        for j in range(CONV_K):
            a, b = divmod(first + j, SUBLANES)
            lo = r * R + a * SUBLANES
            acc = acc + wb_ref[j] * u_ref[b, lo:lo + R, :]
        mu = jnp.mean(acc, axis=-1, keepdims=True)
        d = acc - mu
        var = jnp.mean(d * d, axis=-1, keepdims=True)
        yn = d * lax.rsqrt(var + LN_EPS) * vb_ref[1] + vb_ref[2]
        z_ref[r * R:(r + 1) * R, :] = _silu(yn)
    out = jnp.dot(z_ref[...].astype(BF16), pw_ref[...], preferred_element_type=F32) + pb_ref[...]
    o_ref[...] = (out * _silu(g_ref[...].astype(F32))).astype(BF16)


def conformer_conv(proj, conv_w, conv_b, ln_g, ln_b, pw_w, pw_b):
    S = proj.shape[0]
    T, H = CONV_T, CONV_HALO
    ab, bb, gb = OFF_CA // CONV_CH, OFF_CB // CONV_CH, OFF_GC // CONV_CH
    hpb = T // H
    vec = pl.BlockSpec((1, CONV_CH), lambda i: (0, 0))
    return pl.pallas_call(
        _conv_kernel,
        grid=(S // T,),
        in_specs=[pl.BlockSpec((T, CONV_CH), lambda i: (i, ab)),
                  pl.BlockSpec((T, CONV_CH), lambda i: (i, bb)),
                  pl.BlockSpec((H, CONV_CH), lambda i: (jnp.maximum(i * hpb - 1, 0), ab)),
                  pl.BlockSpec((H, CONV_CH), lambda i: (jnp.maximum(i * hpb - 1, 0), bb)),
                  pl.BlockSpec((T, CONV_CH), lambda i: (i, gb)),
                  pl.BlockSpec((CONV_K, CONV_CH), lambda i: (0, 0)),
                  vec, vec, vec,
                  pl.BlockSpec((CONV_CH, CONV_CH), lambda i: (0, 0)),
                  vec],
        out_specs=pl.BlockSpec((T, CONV_CH), lambda i: (i, 0)),
        out_shape=jax.ShapeDtypeStruct((S, CONV_CH), BF16),
        scratch_shapes=[pltpu.VMEM((SUBLANES, T + H, CONV_CH), F32),
                        pltpu.VMEM((T, CONV_CH), F32),
                        pltpu.VMEM((CONV_K, SUBLANES, CONV_CH), F32),
                        pltpu.VMEM((3, SUBLANES, CONV_CH), F32)],
        compiler_params=_cparams(("arbitrary",)),
        name="conformer_conv",
    )(proj, proj, proj, proj, proj, conv_w, conv_b.reshape(1, -1), ln_g.reshape(1, -1),
      ln_b.reshape(1, -1), pw_w, pw_b.reshape(1, -1))


def _outproj_kernel(ya_ref, yr_ref, yc_ref, w_ref, o_ref):
    a_end = DA_WIDTH
    r_end = DA_WIDTH + RET_WIDTH
    acc = jnp.dot(ya_ref[...], w_ref[0:a_end, :].astype(BF16), preferred_element_type=F32)
    acc = acc + jnp.dot(yr_ref[...], w_ref[a_end:r_end, :].astype(BF16),
                        preferred_element_type=F32)
    acc = acc + jnp.dot(yc_ref[...], w_ref[r_end:, :].astype(BF16),
                        preferred_element_type=F32)
    o_ref[...] = acc


def out_projection(ya, yr, yc, w_all, layer):
    S = ya.shape[0]
    _, K, N = w_all.shape
    once = pl.Buffered(1)
    return pl.pallas_call(
        _outproj_kernel,
        grid=(S // OUT_TM, N // OUT_TN),
        in_specs=[pl.BlockSpec((OUT_TM, DA_WIDTH), lambda i, j: (i, 0), pipeline_mode=once),
                  pl.BlockSpec((OUT_TM, RET_WIDTH), lambda i, j: (i, 0), pipeline_mode=once),
                  pl.BlockSpec((OUT_TM, CONV_CH), lambda i, j: (i, 0), pipeline_mode=once),
                  pl.BlockSpec((None, K, OUT_TN), lambda i, j: (layer, 0, j))],
        out_specs=pl.BlockSpec((OUT_TM, OUT_TN), lambda i, j: (i, j)),
        out_shape=jax.ShapeDtypeStruct((S, N), F32),
        compiler_params=_cparams(("arbitrary", "arbitrary")),
        name="out_projection",
    )(ya, yr, yc, w_all)


def kernel(x, pre_norm_g, w_in, lam_q1, lam_k1, lam_q2, lam_k2, diff_subln_g, conv_w, conv_b,
           conv_ln_g, conv_ln_b, conv_pw_w, conv_pw_b, w_out, post_norm_g):
    B, S, D = x.shape
    assert (B, S, D) == (1, SEQ, D_MODEL)
    xs = x.reshape(S, D)
    rtabs = rope_tables(S)
    ret_tabs = retention_tables()
    h = prenorm(xs, pre_norm_g[0])
    for i in range(DEPTH):
        lam_init = 0.8 - 0.6 * math.exp(-0.3 * i)
        proj, v_t = in_projection(h, w_in, i, rtabs)
        ya = diff_attention(proj, v_t, lam_q1[i], lam_k1[i], lam_q2[i], lam_k2[i],
                            diff_subln_g[i], lam_init)
        yr = retention(proj, ret_tabs)
        yc = conformer_conv(proj, conv_w[i], conv_b[i], conv_ln_g[i], conv_ln_b[i],
                            conv_pw_w[i].astype(BF16), conv_pw_b[i])
        mix = out_projection(ya, yr, yc, w_out, i)
        g_next = pre_norm_g[(i + 1) % DEPTH]
        xs, h = post_norm_residual(mix, xs, post_norm_g[i], g_next)
    return xs.reshape(B, S, D)
```

```python
import functools
import math

import jax
import jax.numpy as jnp
from jax import lax
from jax.experimental import pallas as pl
from jax.experimental.pallas import tpu as pltpu

D_MODEL = 4096
SEQ = 8192
DEPTH = 4
DA_WIDTH = 2048
DA_HEAD = 128
DA_HEADS = 8
DA_VDIM = 2 * DA_HEAD
ROPE_DIM = 32
ROPE_THETA = 500000.0
RET_WIDTH = 1024
RET_HEADS = 4
RET_V_DIM = 256
RET_QK_DIM = 128
RET_QK_WIDTH = 512
RET_THETA = 10000.0
CONV_CH = 1024
CONV_K = 31
CHUNK = 128
IN_COLS = 14336
EPS = 1e-6
LN_EPS = 1e-5

OFF_QA = 0
OFF_KA = 2048
OFF_VA = 4096
OFF_GA = 6144
OFF_QR = 8192
OFF_KR = 8704
OFF_VR = 9216
OFF_GR = 10240
OFF_CA = 11264
OFF_CB = 12288
OFF_GC = 13312

LANES = 128
SUBLANES = 8

PRE_TM = 256
INP_TM = 2048
INP_TN = 512
ATT_T = 512
ATT_TQ = 1024
RET_TM = 1024
CONV_T = 512
CONV_HALO = 32
OUT_TM = 2048
OUT_TN = 512
VMEM_LIMIT = 56 * 1024 * 1024
INP_VMEM_LIMIT = 60 * 1024 * 1024
LOG2E = math.log2(math.e)

F32 = jnp.float32
BF16 = jnp.bfloat16


def _cparams(sem):
    return pltpu.CompilerParams(dimension_semantics=sem, vmem_limit_bytes=VMEM_LIMIT)


def _silu(x):
    return x * jax.nn.sigmoid(x)


def _prenorm_kernel(x_ref, g_ref, h_ref):
    x = x_ref[...]
    r = lax.rsqrt(jnp.mean(x * x, axis=-1, keepdims=True) + EPS)
    h_ref[...] = (x * r * g_ref[...]).astype(BF16)


def prenorm(x, g):
    S, D = x.shape
    return pl.pallas_call(
        _prenorm_kernel,
        grid=(S // PRE_TM,),
        in_specs=[pl.BlockSpec((PRE_TM, D), lambda i: (i, 0)),
                  pl.BlockSpec((1, D), lambda i: (0, 0))],
        out_specs=pl.BlockSpec((PRE_TM, D), lambda i: (i, 0)),
        out_shape=jax.ShapeDtypeStruct((S, D), BF16),
        compiler_params=_cparams(("arbitrary",)),
        name="prenorm",
    )(x, g.reshape(1, D))


def _post_kernel(mix_ref, x_ref, gp_ref, gn_ref, xo_ref, h_ref):
    mix = mix_ref[...]
    r = lax.rsqrt(jnp.mean(mix * mix, axis=-1, keepdims=True) + EPS)
    xn = x_ref[...] + mix * r * gp_ref[...]
    xo_ref[...] = xn
    r2 = lax.rsqrt(jnp.mean(xn * xn, axis=-1, keepdims=True) + EPS)
    h_ref[...] = (xn * r2 * gn_ref[...]).astype(BF16)


def post_norm_residual(mix, x, g_post, g_next):
    S, D = x.shape
    row = pl.BlockSpec((PRE_TM, D), lambda i: (i, 0))
    vec = pl.BlockSpec((1, D), lambda i: (0, 0))
    return pl.pallas_call(
        _post_kernel,
        grid=(S // PRE_TM,),
        in_specs=[row, row, vec, vec],
        out_specs=[row, row],
        out_shape=[jax.ShapeDtypeStruct((S, D), F32), jax.ShapeDtypeStruct((S, D), BF16)],
        compiler_params=_cparams(("arbitrary",)),
        name="post_norm_residual",
    )(mix, x, g_post.reshape(1, D), g_next.reshape(1, D))


EPI_PLAIN, EPI_QA, EPI_KA, EPI_QR, EPI_KR = range(5)


def _epilogue_kind(col0):
    is_qr = jnp.logical_and(col0 >= OFF_QR, col0 < OFF_KR)
    is_kr = jnp.logical_and(col0 >= OFF_KR, col0 < OFF_VR)
    return jnp.where(col0 < OFF_KA, EPI_QA,
                     jnp.where(col0 < OFF_VA, EPI_KA,
                               jnp.where(is_qr, EPI_QR, jnp.where(is_kr, EPI_KR, EPI_PLAIN))))


def _inproj_kernel(h_ref, w_ref, tab_ref, o_ref, vt_ref, acc0_ref, acc1_ref):
    t = pl.program_id(0)
    n_j = IN_COLS // INP_TN
    col0 = (jnp.maximum(t - 1, 0) % n_j) * INP_TN
    is_diff = col0 < OFF_VA
    is_ret = jnp.logical_and(col0 >= OFF_QR, col0 < OFF_VR)
    half = ROPE_DIM // 2
    shift1 = jnp.where(is_diff, LANES - half, jnp.where(is_ret, RET_QK_DIM // 2, 0))
    shift2 = jnp.where(is_diff, half, 0)

    @pl.when(t == 0)
    def _():
        acc1_ref[...] = jnp.zeros(acc1_ref.shape, F32)

    def step(acc_w, acc_r, rotary):
        acc_w[...] = jnp.dot(h_ref[...], w_ref[...].astype(BF16), preferred_element_type=F32)
        if rotary:
            c, s1, s2 = tab_ref[0], tab_ref[1], tab_ref[2]
            for g in range(INP_TN // LANES):
                xg = acc_r[:, g * LANES:(g + 1) * LANES]
                y = xg * c + pltpu.roll(xg, shift1, 1) * s1 + pltpu.roll(xg, shift2, 1) * s2
                o_ref[:, g * LANES:(g + 1) * LANES] = y.astype(BF16)
        else:
            o_ref[...] = acc_r[...].astype(BF16)

            @pl.when(jnp.logical_and(col0 >= OFF_VA, col0 < OFF_GA))
            def _():
                acc_t = acc_r[...].T
                for b in range(INP_TM // ATT_T):
                    vt_ref[b] = acc_t[:, b * ATT_T:(b + 1) * ATT_T].astype(BF16)

    even = t % 2 == 0
    rot = jnp.logical_or(is_diff, is_ret)
    for parity, (acc_w, acc_r) in ((even, (acc0_ref, acc1_ref)),
                                   (jnp.logical_not(even), (acc1_ref, acc0_ref))):
        for cond, rotary in ((rot, True), (jnp.logical_not(rot), False)):
            pl.when(jnp.logical_and(parity, cond))(
                functools.partial(step, acc_w, acc_r, rotary))


def in_projection(h, w_all, layer, tabs):
    S, D = h.shape
    N = w_all.shape[2]
    for off in (OFF_KA, OFF_VA, OFF_GA, OFF_QR, OFF_KR, OFF_VR):
        assert off % INP_TN == 0
    n_i, n_j = S // INP_TM, N // INP_TN
    n_tiles = n_i * n_j
    va_first = OFF_VA // INP_TN
    va_last = OFF_GA // INP_TN - 1

    def cur(t):
        tc = jnp.minimum(t, n_tiles - 1)
        return tc // n_j, tc % n_j

    def prev(t):
        tp = jnp.maximum(t - 1, 0)
        return tp // n_j, tp % n_j

    def vt_index(t):
        i, j = prev(t)
        return i, jnp.clip(j, va_first, va_last) - va_first, 0

    def tab_index(t):
        i, j = prev(t)
        return _epilogue_kind(j * INP_TN), 0, i, 0

    return pl.pallas_call(
        _inproj_kernel,
        grid=(n_tiles + 1,),
        in_specs=[pl.BlockSpec((INP_TM, D), lambda t: (cur(t)[0], 0),
                               pipeline_mode=pl.Buffered(1)),
                  pl.BlockSpec((None, D, INP_TN), lambda t: (layer, 0, cur(t)[1])),
                  pl.BlockSpec((None, 3, INP_TM, LANES), tab_index)],
        out_specs=[pl.BlockSpec((INP_TM, INP_TN), prev),
                   pl.BlockSpec((INP_TM // ATT_T, INP_TN, ATT_T), vt_index,
                                pipeline_mode=pl.Buffered(1))],
        out_shape=[jax.ShapeDtypeStruct((S, N), BF16),
                   jax.ShapeDtypeStruct((S // ATT_T, DA_WIDTH, ATT_T), BF16)],
        scratch_shapes=[pltpu.VMEM((INP_TM, INP_TN), F32), pltpu.VMEM((INP_TM, INP_TN), F32)],
        compiler_params=pltpu.CompilerParams(dimension_semantics=("arbitrary",),
                                             vmem_limit_bytes=INP_VMEM_LIMIT),
        name="in_projection",
    )(h, w_all, tabs)


def rope_tables(S):
    pos = jnp.arange(S, dtype=F32)
    half = ROPE_DIM // 2
    inv = 1.0 / (ROPE_THETA ** (jnp.arange(0, ROPE_DIM, 2, dtype=F32) / ROPE_DIM))
    ang = pos[:, None] * inv[None, :]
    cos, sin = jnp.cos(ang), jnp.sin(ang)
    one = jnp.ones((S, LANES), F32)
    zero = jnp.zeros((S, LANES), F32)
    ca = jnp.concatenate([cos, cos, one[:, ROPE_DIM:]], axis=1)
    sa = jnp.concatenate([-sin, zero[:, half:]], axis=1)
    sb = jnp.concatenate([zero[:, :half], sin, zero[:, ROPE_DIM:]], axis=1)
    inv_r = 1.0 / (RET_THETA ** (jnp.arange(0, RET_QK_DIM, 2, dtype=F32) / RET_QK_DIM))
    ang_r = pos[:, None] * inv_r[None, :]
    cos_r, sin_r = jnp.cos(ang_r), jnp.sin(ang_r)
    cr = jnp.concatenate([cos_r, cos_r], axis=1)
    sr = jnp.concatenate([-sin_r, sin_r], axis=1)
    q_sc = DA_HEAD ** -0.5 * LOG2E
    k_sc = RET_QK_DIM ** -0.5
    kinds = {EPI_PLAIN: (one, zero, zero),
             EPI_QA: (ca * q_sc, sa * q_sc, sb * q_sc),
             EPI_KA: (ca, sa, sb),
             EPI_QR: (cr, sr, zero),
             EPI_KR: (cr * k_sc, sr * k_sc, zero)}
    return jnp.stack([jnp.stack(kinds[k]) for k in range(len(kinds))])


def _attn_kernel(lam_init, q_ref, k_ref, vt_ref, g_ref, lq1_ref, lk1_ref, lq2_ref, lk2_ref,
                 sg_ref, o_ref, acc_ref, sa_ref, sb_ref):
    TQ, TK = ATT_TQ, ATT_T
    NT = (((1,), (1,)), ((), ()))
    qi = pl.program_id(1)
    acc_ref[...] = jnp.zeros(acc_ref.shape, F32)

    def scores(kb, s_ref, cols, diag):
        start = pl.multiple_of(kb * TK, TK)
        n = cols.stop - cols.start
        mb = []
        for c in range(2):
            k = k_ref[pl.ds(start, TK), c * DA_HEAD:(c + 1) * DA_HEAD]
            q = q_ref[cols, c * DA_HEAD:(c + 1) * DA_HEAD]
            s = lax.dot_general(k, q, NT, preferred_element_type=F32)
            if diag:
                key = lax.broadcasted_iota(jnp.int32, (TK, n), 0)
                qry = lax.broadcasted_iota(jnp.int32, (TK, n), 1)
                s = jnp.where(key <= qry, s, -jnp.inf)
            s_ref[c, :, cols] = s
            mb.append(jnp.max(s, axis=0, keepdims=True))
        return tuple(mb)

    def softmax_pv(kb, s_ref, mb, stats, cols, diag_cols=0):
        vt = vt_ref[kb]
        n = cols.stop - cols.start
        out = []
        for c in range(2):
            m_prev, l_prev = stats[2 * c][:, cols], stats[2 * c + 1][:, cols]
            s = s_ref[c, :, cols]
            m_blk = mb[c] if mb is not None else None
            if diag_cols:
                key = lax.broadcasted_iota(jnp.int32, (TK, n), 0)
                qry = lax.broadcasted_iota(jnp.int32, (TK, n), 1)
                s = jnp.where(jnp.logical_or(key <= qry, qry >= diag_cols), s, -jnp.inf)
                m_blk = jnp.max(s, axis=0, keepdims=True)
            m_new = jnp.maximum(m_prev, m_blk)
            alpha = jnp.exp2(m_prev - m_new)
            p = jnp.exp2(s - m_new)
            l_new = alpha * l_prev + jnp.sum(p, axis=0, keepdims=True)
            acc_ref[c, :, cols] = alpha * acc_ref[c, :, cols] + jnp.dot(
                vt, p.astype(BF16), preferred_element_type=F32)
            if n == TQ:
                out += [m_new, l_new]
            else:
                out += [jnp.concatenate([stats[2 * c][:, :cols.start], m_new], axis=1),
                        jnp.concatenate([stats[2 * c + 1][:, :cols.start], l_new], axis=1)]
        return tuple(out)

    full = slice(0, TQ)
    late = slice(TQ - TK, TQ)
    neg = jnp.full((1, TQ), -jnp.inf, F32)
    zero = jnp.zeros((1, TQ), F32)

    def body(j, carry):
        stats, mb_a = carry[:4], carry[4:]
        mb_b = scores(2 * j + 1, sb_ref, full, False)
        stats = softmax_pv(2 * j, sa_ref, mb_a, stats, full)
        mb_a = scores(2 * j + 2, sa_ref, full, False)
        stats = softmax_pv(2 * j + 1, sb_ref, mb_b, stats, full)
        return stats + mb_a

    mb_a = scores(0, sa_ref, full, False)
    carry = lax.fori_loop(0, qi, body, (neg, zero, neg, zero) + mb_a)
    stats = carry[:4]
    mb_b = scores(2 * qi + 1, sb_ref, late, True)
    stats = softmax_pv(2 * qi, sa_ref, None, stats, full, diag_cols=TK)
    _, l0, _, l1 = softmax_pv(2 * qi + 1, sb_ref, mb_b, stats, late)

    lam = (jnp.exp(jnp.sum(lq1_ref[...] * lk1_ref[...], axis=-1, keepdims=True))
           - jnp.exp(jnp.sum(lq2_ref[...] * lk2_ref[...], axis=-1, keepdims=True))
           + lam_init)
    o_t = acc_ref[0] * (1.0 / l0) - lam * (acc_ref[1] * (1.0 / l1))
    o = o_t.T
    r = lax.rsqrt(jnp.mean(o * o, axis=-1, keepdims=True) + EPS)
    y = o * r * sg_ref[...] * (1.0 - lam_init)
    o_ref[...] = (y * _silu(g_ref[...].astype(F32))).astype(BF16)


def diff_attention(proj, v_t, lq1, lk1, lq2, lk2, subln_g, lam_init):
    S = proj.shape[0]
    TQ, TK = ATT_TQ, ATT_T
    assert TQ == 2 * TK
    qb, kb, gb = OFF_QA // DA_VDIM, OFF_KA // DA_VDIM, OFF_GA // DA_VDIM
    vec = pl.BlockSpec((1, DA_HEAD), lambda h, i: (0, 0))
    return pl.pallas_call(
        functools.partial(_attn_kernel, lam_init),
        grid=(DA_HEADS, S // TQ),
        in_specs=[pl.BlockSpec((TQ, DA_VDIM), lambda h, i: (i, qb + h)),
                  pl.BlockSpec((S, DA_VDIM), lambda h, i: (0, kb + h)),
                  pl.BlockSpec((S // TK, DA_VDIM, TK), lambda h, i: (0, h, 0)),
                  pl.BlockSpec((TQ, DA_VDIM), lambda h, i: (i, gb + h)),
                  vec, vec, vec, vec,
                  pl.BlockSpec((1, DA_VDIM), lambda h, i: (0, 0))],
        out_specs=pl.BlockSpec((TQ, DA_VDIM), lambda h, i: (i, h)),
        out_shape=jax.ShapeDtypeStruct((S, DA_WIDTH), BF16),
        scratch_shapes=[pltpu.VMEM((2, DA_VDIM, TQ), F32),
                        pltpu.VMEM((2, TK, TQ), F32), pltpu.VMEM((2, TK, TQ), F32)],
        compiler_params=_cparams(("arbitrary", "arbitrary")),
        name="diff_attention",
    )(proj, proj, v_t, proj, lq1.reshape(1, -1), lk1.reshape(1, -1), lq2.reshape(1, -1),
      lk2.reshape(1, -1), subln_g.reshape(1, -1))


def _ret_kernel(q_ref, k_ref, v_ref, g_ref, dm_ref, zeta_ref, xi_ref, cd_ref, o_ref, st_ref):
    @pl.when(pl.program_id(1) == 0)
    def _():
        st_ref[...] = jnp.zeros(st_ref.shape, F32)

    dmat = dm_ref[0]
    zeta = zeta_ref[0]
    xi = xi_ref[0]
    cdecay = cd_ref[0]
    for n in range(RET_TM // CHUNK):
        rows = slice(n * CHUNK, (n + 1) * CHUNK)
        qc = q_ref[rows, :]
        kc = k_ref[rows, :]
        vc = v_ref[rows, :]
        state = st_ref[...]
        inner = lax.dot_general(qc, kc, (((1,), (1,)), ((), ())),
                                preferred_element_type=F32) * dmat
        o = jnp.dot(inner.astype(BF16), vc, preferred_element_type=F32)
        o = o + jnp.dot((qc.astype(F32) * xi).astype(BF16), state.astype(BF16),
                        preferred_element_type=F32)
        kz = (kc.astype(F32) * zeta).astype(BF16)
        st_ref[...] = state * cdecay + lax.dot_general(
            kz, vc, (((0,), (0,)), ((), ())), preferred_element_type=F32)
        r = lax.rsqrt(jnp.mean(o * o, axis=-1, keepdims=True) + EPS)
        o_ref[rows, :] = (o * r * _silu(g_ref[rows, :].astype(F32))).astype(BF16)


def retention_tables():
    lg = jnp.log(1.0 - 2.0 ** (-5.0 - jnp.arange(RET_HEADS, dtype=F32)))
    idx = jnp.arange(CHUNK, dtype=F32)
    rel = idx[:, None] - idx[None, :]
    dmat = jnp.where(rel >= 0, jnp.exp(jnp.maximum(rel, 0.0) * lg[:, None, None]), 0.0)
    zeta = jnp.exp((CHUNK - 1 - idx)[None, :] * lg[:, None])[..., None]
    xi = jnp.exp((idx + 1.0)[None, :] * lg[:, None])[..., None]
    cdecay = jnp.exp(CHUNK * lg).reshape(RET_HEADS, 1, 1)
    return dmat, zeta, xi, cdecay


def retention(proj, tabs):
    S = proj.shape[0]
    qb, kb = OFF_QR // RET_QK_DIM, OFF_KR // RET_QK_DIM
    vb, gb = OFF_VR // RET_V_DIM, OFF_GR // RET_V_DIM
    dmat, zeta, xi, cdecay = tabs
    return pl.pallas_call(
        _ret_kernel,
        grid=(RET_HEADS, S // RET_TM),
        in_specs=[pl.BlockSpec((RET_TM, RET_QK_DIM), lambda h, i: (i, qb + h)),
                  pl.BlockSpec((RET_TM, RET_QK_DIM), lambda h, i: (i, kb + h)),
                  pl.BlockSpec((RET_TM, RET_V_DIM), lambda h, i: (i, vb + h)),
                  pl.BlockSpec((RET_TM, RET_V_DIM), lambda h, i: (i, gb + h)),
                  pl.BlockSpec((1, CHUNK, CHUNK), lambda h, i: (h, 0, 0)),
                  pl.BlockSpec((1, CHUNK, 1), lambda h, i: (h, 0, 0)),
                  pl.BlockSpec((1, CHUNK, 1), lambda h, i: (h, 0, 0)),
                  pl.BlockSpec((1, 1, 1), lambda h, i: (h, 0, 0))],
        out_specs=pl.BlockSpec((RET_TM, RET_V_DIM), lambda h, i: (i, h)),
        out_shape=jax.ShapeDtypeStruct((S, RET_WIDTH), BF16),
        scratch_shapes=[pltpu.VMEM((RET_QK_DIM, RET_V_DIM), F32)],
        compiler_params=_cparams(("arbitrary", "arbitrary")),
        name="retention",
    )(proj, proj, proj, proj, dmat, zeta, xi, cdecay)


def _conv_kernel(a_ref, b_ref, ha_ref, hb_ref, g_ref, w_ref, cb_ref, lg_ref, lb_ref,
                 pw_ref, pb_ref, o_ref, u_ref, z_ref, wb_ref, vb_ref):
    i = pl.program_id(0)
    T, H, R = CONV_T, CONV_HALO, SUBLANES
    for j in range(CONV_K):
        wb_ref[j] = jnp.broadcast_to(w_ref[j:j + 1, :], (R, CONV_CH))
    for n, ref in enumerate((cb_ref, lg_ref, lb_ref)):
        vb_ref[n] = jnp.broadcast_to(ref[...], (R, CONV_CH))
    keep = (i > 0).astype(F32)
    u_ref[0, 0:H, :] = (ha_ref[...].astype(F32) * jax.nn.sigmoid(hb_ref[...].astype(F32))
                        * keep)
    u_ref[0, H:H + T, :] = a_ref[...].astype(F32) * jax.nn.sigmoid(b_ref[...].astype(F32))
    u_all = u_ref[0]
    for b in range(1, SUBLANES):
        u_ref[b] = pltpu.roll(u_all, T + H - b, 0)
    first = H - (CONV_K - 1)
    for r in range(T // R):
        acc = vb_ref[0]
        for j in range(CONV_K):
            a, b = divmod(first + j, SUBLANES)
            lo = r * R + a * SUBLANES
            acc = acc + wb_ref[j] * u_ref[b, lo:lo + R, :]
        mu = jnp.mean(acc, axis=-1, keepdims=True)
        d = acc - mu
        var = jnp.mean(d * d, axis=-1, keepdims=True)
        yn = d * lax.rsqrt(var + LN_EPS) * vb_ref[1] + vb_ref[2]
        z_ref[r * R:(r + 1) * R, :] = _silu(yn)
    out = jnp.dot(z_ref[...].astype(BF16), pw_ref[...], preferred_element_type=F32) + pb_ref[...]
    o_ref[...] = (out * _silu(g_ref[...].astype(F32))).astype(BF16)


def conformer_conv(proj, conv_w, conv_b, ln_g, ln_b, pw_w, pw_b):
    S = proj.shape[0]
    T, H = CONV_T, CONV_HALO
    ab, bb, gb = OFF_CA // CONV_CH, OFF_CB // CONV_CH, OFF_GC // CONV_CH
    hpb = T // H
    vec = pl.BlockSpec((1, CONV_CH), lambda i: (0, 0))
    return pl.pallas_call(
        _conv_kernel,
        grid=(S // T,),
        in_specs=[pl.BlockSpec((T, CONV_CH), lambda i: (i, ab)),
                  pl.BlockSpec((T, CONV_CH), lambda i: (i, bb)),
                  pl.BlockSpec((H, CONV_CH), lambda i: (jnp.maximum(i * hpb - 1, 0), ab)),
                  pl.BlockSpec((H, CONV_CH), lambda i: (jnp.maximum(i * hpb - 1, 0), bb)),
                  pl.BlockSpec((T, CONV_CH), lambda i: (i, gb)),
                  pl.BlockSpec((CONV_K, CONV_CH), lambda i: (0, 0)),
                  vec, vec, vec,
                  pl.BlockSpec((CONV_CH, CONV_CH), lambda i: (0, 0)),
                  vec],
        out_specs=pl.BlockSpec((T, CONV_CH), lambda i: (i, 0)),
        out_shape=jax.ShapeDtypeStruct((S, CONV_CH), BF16),
        scratch_shapes=[pltpu.VMEM((SUBLANES, T + H, CONV_CH), F32),
                        pltpu.VMEM((T, CONV_CH), F32),
                        pltpu.VMEM((CONV_K, SUBLANES, CONV_CH), F32),
                        pltpu.VMEM((3, SUBLANES, CONV_CH), F32)],
        compiler_params=_cparams(("arbitrary",)),
        name="conformer_conv",
    )(proj, proj, proj, proj, proj, conv_w, conv_b.reshape(1, -1), ln_g.reshape(1, -1),
      ln_b.reshape(1, -1), pw_w, pw_b.reshape(1, -1))


def _outproj_kernel(ya_ref, yr_ref, yc_ref, w_ref, o_ref):
    a_end = DA_WIDTH
    r_end = DA_WIDTH + RET_WIDTH
    acc = jnp.dot(ya_ref[...], w_ref[0:a_end, :].astype(BF16), preferred_element_type=F32)
    acc = acc + jnp.dot(yr_ref[...], w_ref[a_end:r_end, :].astype(BF16),
                        preferred_element_type=F32)
    acc = acc + jnp.dot(yc_ref[...], w_ref[r_end:, :].astype(BF16),
                        preferred_element_type=F32)
    o_ref[...] = acc


def out_projection(ya, yr, yc, w_all, layer):
    S = ya.shape[0]
    _, K, N = w_all.shape
    once = pl.Buffered(1)
    return pl.pallas_call(
        _outproj_kernel,
        grid=(S // OUT_TM, N // OUT_TN),
        in_specs=[pl.BlockSpec((OUT_TM, DA_WIDTH), lambda i, j: (i, 0), pipeline_mode=once),
                  pl.BlockSpec((OUT_TM, RET_WIDTH), lambda i, j: (i, 0), pipeline_mode=once),
                  pl.BlockSpec((OUT_TM, CONV_CH), lambda i, j: (i, 0), pipeline_mode=once),
                  pl.BlockSpec((None, K, OUT_TN), lambda i, j: (layer, 0, j))],
        out_specs=pl.BlockSpec((OUT_TM, OUT_TN), lambda i, j: (i, j)),
        out_shape=jax.ShapeDtypeStruct((S, N), F32),
        compiler_params=_cparams(("arbitrary", "arbitrary")),
        name="out_projection",
    )(ya, yr, yc, w_all)


def kernel(x, pre_norm_g, w_in, lam_q1, lam_k1, lam_q2, lam_k2, diff_subln_g, conv_w, conv_b,
           conv_ln_g, conv_ln_b, conv_pw_w, conv_pw_b, w_out, post_norm_g):
    B, S, D = x.shape
    assert (B, S, D) == (1, SEQ, D_MODEL)
    xs = x.reshape(S, D)
    rtabs = rope_tables(S)
    ret_tabs = retention_tables()
    h = prenorm(xs, pre_norm_g[0])
    for i in range(DEPTH):
        lam_init = 0.8 - 0.6 * math.exp(-0.3 * i)
        proj, v_t = in_projection(h, w_in, i, rtabs)
        ya = diff_attention(proj, v_t, lam_q1[i], lam_k1[i], lam_q2[i], lam_k2[i],
                            diff_subln_g[i], lam_init)
        yr = retention(proj, ret_tabs)
        yc = conformer_conv(proj, conv_w[i], conv_b[i], conv_ln_g[i], conv_ln_b[i],
                            conv_pw_w[i].astype(BF16), conv_pw_b[i])
        mix = out_projection(ya, yr, yc, w_out, i)
        g_next = pre_norm_g[(i + 1) % DEPTH]
        xs, h = post_norm_residual(mix, xs, post_norm_g[i], g_next)
    return xs.reshape(B, S, D)
```

```python
import functools
import math

import jax
import jax.numpy as jnp
from jax import lax
from jax.experimental import pallas as pl
from jax.experimental.pallas import tpu as pltpu

D_MODEL = 4096
SEQ = 8192
DEPTH = 4
DA_WIDTH = 2048
DA_HEAD = 128
DA_HEADS = 8
DA_VDIM = 2 * DA_HEAD
ROPE_DIM = 32
ROPE_THETA = 500000.0
RET_WIDTH = 1024
RET_HEADS = 4
RET_V_DIM = 256
RET_QK_DIM = 128
RET_QK_WIDTH = 512
RET_THETA = 10000.0
CONV_CH = 1024
CONV_K = 31
CHUNK = 128
IN_COLS = 14336
EPS = 1e-6
LN_EPS = 1e-5

OFF_QA = 0
OFF_KA = 2048
OFF_VA = 4096
OFF_GA = 6144
OFF_QR = 8192
OFF_KR = 8704
OFF_VR = 9216
OFF_GR = 10240
OFF_CA = 11264
OFF_CB = 12288
OFF_GC = 13312

LANES = 128
SUBLANES = 8

PRE_TM = 256
INP_TM = 1024
INP_TN = 512
ATT_T = 512
ATT_TQ = 1024
RET_TM = 1024
CONV_T = 512
CONV_HALO = 32
OUT_TM = 2048
OUT_TN = 512
VMEM_LIMIT = 56 * 1024 * 1024
LOG2E = math.log2(math.e)

F32 = jnp.float32
BF16 = jnp.bfloat16


def _cparams(sem):
    return pltpu.CompilerParams(dimension_semantics=sem, vmem_limit_bytes=VMEM_LIMIT)


def _silu(x):
    return x * jax.nn.sigmoid(x)


def _prenorm_kernel(x_ref, g_ref, h_ref):
    x = x_ref[...]
    r = lax.rsqrt(jnp.mean(x * x, axis=-1, keepdims=True) + EPS)
    h_ref[...] = (x * r * g_ref[...]).astype(BF16)


def prenorm(x, g):
    S, D = x.shape
    return pl.pallas_call(
        _prenorm_kernel,
        grid=(S // PRE_TM,),
        in_specs=[pl.BlockSpec((PRE_TM, D), lambda i: (i, 0)),
                  pl.BlockSpec((1, D), lambda i: (0, 0))],
        out_specs=pl.BlockSpec((PRE_TM, D), lambda i: (i, 0)),
        out_shape=jax.ShapeDtypeStruct((S, D), BF16),
        compiler_params=_cparams(("arbitrary",)),
        name="prenorm",
    )(x, g.reshape(1, D))


def _post_kernel(mix_ref, x_ref, gp_ref, gn_ref, xo_ref, h_ref):
    mix = mix_ref[...].astype(F32)
    r = lax.rsqrt(jnp.mean(mix * mix, axis=-1, keepdims=True) + EPS)
    xn = x_ref[...] + mix * r * gp_ref[...]
    xo_ref[...] = xn
    r2 = lax.rsqrt(jnp.mean(xn * xn, axis=-1, keepdims=True) + EPS)
    h_ref[...] = (xn * r2 * gn_ref[...]).astype(BF16)


def post_norm_residual(mix, x, g_post, g_next):
    S, D = x.shape
    row = pl.BlockSpec((PRE_TM, D), lambda i: (i, 0))
    vec = pl.BlockSpec((1, D), lambda i: (0, 0))
    return pl.pallas_call(
        _post_kernel,
        grid=(S // PRE_TM,),
        in_specs=[row, row, vec, vec],
        out_specs=[row, row],
        out_shape=[jax.ShapeDtypeStruct((S, D), F32), jax.ShapeDtypeStruct((S, D), BF16)],
        compiler_params=_cparams(("arbitrary",)),
        name="post_norm_residual",
    )(mix, x, g_post.reshape(1, D), g_next.reshape(1, D))


EPI_PLAIN, EPI_QA, EPI_KA, EPI_QR, EPI_KR = range(5)


def _epilogue_kind(col0):
    is_qr = jnp.logical_and(col0 >= OFF_QR, col0 < OFF_KR)
    is_kr = jnp.logical_and(col0 >= OFF_KR, col0 < OFF_VR)
    return jnp.where(col0 < OFF_KA, EPI_QA,
                     jnp.where(col0 < OFF_VA, EPI_KA,
                               jnp.where(is_qr, EPI_QR, jnp.where(is_kr, EPI_KR, EPI_PLAIN))))


def _inproj_kernel(h_ref, w_ref, tab_ref, o_ref, vt_ref, acc0_ref, acc1_ref):
    t = pl.program_id(0)
    n_j = IN_COLS // INP_TN
    col0 = (jnp.maximum(t - 1, 0) % n_j) * INP_TN
    is_diff = col0 < OFF_VA
    is_ret = jnp.logical_and(col0 >= OFF_QR, col0 < OFF_VR)
    half = ROPE_DIM // 2
    shift1 = jnp.where(is_diff, LANES - half, jnp.where(is_ret, RET_QK_DIM // 2, 0))
    shift2 = jnp.where(is_diff, half, 0)

    @pl.when(t == 0)
    def _():
        acc1_ref[...] = jnp.zeros(acc1_ref.shape, F32)

    def step(acc_w, acc_r, rotary):
        acc_w[...] = jnp.dot(h_ref[...], w_ref[...].astype(BF16), preferred_element_type=F32)
        if rotary:
            c, s1, s2 = tab_ref[0], tab_ref[1], tab_ref[2]
            for g in range(INP_TN // LANES):
                xg = acc_r[:, g * LANES:(g + 1) * LANES]
                y = xg * c + pltpu.roll(xg, shift1, 1) * s1 + pltpu.roll(xg, shift2, 1) * s2
                o_ref[:, g * LANES:(g + 1) * LANES] = y.astype(BF16)
        else:
            o_ref[...] = acc_r[...].astype(BF16)

            @pl.when(jnp.logical_and(col0 >= OFF_VA, col0 < OFF_GA))
            def _():
                acc_t = acc_r[...].T
                for b in range(INP_TM // ATT_T):
                    vt_ref[b] = acc_t[:, b * ATT_T:(b + 1) * ATT_T].astype(BF16)

    even = t % 2 == 0
    rot = jnp.logical_or(is_diff, is_ret)
    for parity, (acc_w, acc_r) in ((even, (acc0_ref, acc1_ref)),
                                   (jnp.logical_not(even), (acc1_ref, acc0_ref))):
        for cond, rotary in ((rot, True), (jnp.logical_not(rot), False)):
            pl.when(jnp.logical_and(parity, cond))(
                functools.partial(step, acc_w, acc_r, rotary))


def in_projection(h, w_all, layer, tabs):
    S, D = h.shape
    N = w_all.shape[2]
    for off in (OFF_KA, OFF_VA, OFF_GA, OFF_QR, OFF_KR, OFF_VR):
        assert off % INP_TN == 0
    n_i, n_j = S // INP_TM, N // INP_TN
    n_tiles = n_i * n_j
    va_first = OFF_VA // INP_TN
    va_last = OFF_GA // INP_TN - 1

    def cur(t):
        tc = jnp.minimum(t, n_tiles - 1)
        return tc // n_j, tc % n_j

    def prev(t):
        tp = jnp.maximum(t - 1, 0)
        return tp // n_j, tp % n_j

    def vt_index(t):
        i, j = prev(t)
        return i, jnp.clip(j, va_first, va_last) - va_first, 0

    def tab_index(t):
        i, j = prev(t)
        return _epilogue_kind(j * INP_TN), 0, i, 0

    return pl.pallas_call(
        _inproj_kernel,
        grid=(n_tiles + 1,),
        in_specs=[pl.BlockSpec((INP_TM, D), lambda t: (cur(t)[0], 0)),
                  pl.BlockSpec((None, D, INP_TN), lambda t: (layer, 0, cur(t)[1])),
                  pl.BlockSpec((None, 3, INP_TM, LANES), tab_index)],
        out_specs=[pl.BlockSpec((INP_TM, INP_TN), prev),
                   pl.BlockSpec((INP_TM // ATT_T, INP_TN, ATT_T), vt_index)],
        out_shape=[jax.ShapeDtypeStruct((S, N), BF16),
                   jax.ShapeDtypeStruct((S // ATT_T, DA_WIDTH, ATT_T), BF16)],
        scratch_shapes=[pltpu.VMEM((INP_TM, INP_TN), F32), pltpu.VMEM((INP_TM, INP_TN), F32)],
        compiler_params=_cparams(("arbitrary",)),
        name="in_projection",
    )(h, w_all, tabs)


def rope_tables(S):
    pos = jnp.arange(S, dtype=F32)
    half = ROPE_DIM // 2
    inv = 1.0 / (ROPE_THETA ** (jnp.arange(0, ROPE_DIM, 2, dtype=F32) / ROPE_DIM))
    ang = pos[:, None] * inv[None, :]
    cos, sin = jnp.cos(ang), jnp.sin(ang)
    one = jnp.ones((S, LANES), F32)
    zero = jnp.zeros((S, LANES), F32)
    ca = jnp.concatenate([cos, cos, one[:, ROPE_DIM:]], axis=1)
    sa = jnp.concatenate([-sin, zero[:, half:]], axis=1)
    sb = jnp.concatenate([zero[:, :half], sin, zero[:, ROPE_DIM:]], axis=1)
    inv_r = 1.0 / (RET_THETA ** (jnp.arange(0, RET_QK_DIM, 2, dtype=F32) / RET_QK_DIM))
    ang_r = pos[:, None] * inv_r[None, :]
    cos_r, sin_r = jnp.cos(ang_r), jnp.sin(ang_r)
    cr = jnp.concatenate([cos_r, cos_r], axis=1)
    sr = jnp.concatenate([-sin_r, sin_r], axis=1)
    q_sc = DA_HEAD ** -0.5 * LOG2E
    k_sc = RET_QK_DIM ** -0.5
    kinds = {EPI_PLAIN: (one, zero, zero),
             EPI_QA: (ca * q_sc, sa * q_sc, sb * q_sc),
             EPI_KA: (ca, sa, sb),
             EPI_QR: (cr, sr, zero),
             EPI_KR: (cr * k_sc, sr * k_sc, zero)}
    return jnp.stack([jnp.stack(kinds[k]) for k in range(len(kinds))])


def _attn_kernel(lam_init, q_ref, k_ref, vt_ref, g_ref, qn_ref, k0n_ref, lq1_ref, lk1_ref,
                 lq2_ref, lk2_ref, sg_ref, o_ref, acc_ref, sa_ref, sb_ref, mb_ref):
    TQ, TK = ATT_TQ, ATT_T
    NT = (((1,), (1,)), ((), ()))
    qi = pl.program_id(1)
    acc_ref[...] = jnp.zeros(acc_ref.shape, F32)

    def scores(kb, s_ref, cols, diag, upcoming=False):
        start = pl.multiple_of(kb * TK, TK)
        n = cols.stop - cols.start
        mb = []
        for c in range(2):
            comp = slice(c * DA_HEAD, (c + 1) * DA_HEAD)
            k = k0n_ref[:, comp] if upcoming else k_ref[pl.ds(start, TK), comp]
            q = qn_ref[cols, comp] if upcoming else q_ref[cols, comp]
            s = lax.dot_general(k, q, NT, preferred_element_type=F32)
            if diag:
                key = lax.broadcasted_iota(jnp.int32, (TK, n), 0)
                qry = lax.broadcasted_iota(jnp.int32, (TK, n), 1)
                s = jnp.where(key <= qry, s, -jnp.inf)
            s_ref[c, :, cols] = s
            mb.append(jnp.max(s, axis=0, keepdims=True))
        return tuple(mb)

    def softmax_pv(kb, s_ref, mb, stats, cols, diag_cols=0):
        vt = vt_ref[kb]
        n = cols.stop - cols.start
        out = []
        for c in range(2):
            m_prev, l_prev = stats[2 * c][:, cols], stats[2 * c + 1][:, cols]
            s = s_ref[c, :, cols]
            m_blk = mb[c] if mb is not None else None
            if diag_cols:
                key = lax.broadcasted_iota(jnp.int32, (TK, n), 0)
                qry = lax.broadcasted_iota(jnp.int32, (TK, n), 1)
                s = jnp.where(jnp.logical_or(key <= qry, qry >= diag_cols), s, -jnp.inf)
                m_blk = jnp.max(s, axis=0, keepdims=True)
            m_new = jnp.maximum(m_prev, m_blk)
            alpha = jnp.exp2(m_prev - m_new)
            p = jnp.exp2(s - m_new)
            l_new = alpha * l_prev + jnp.sum(p, axis=0, keepdims=True)
            acc_ref[c, :, cols] = alpha * acc_ref[c, :, cols] + jnp.dot(
                vt, p.astype(BF16), preferred_element_type=F32)
            if n == TQ:
                out += [m_new, l_new]
            else:
                out += [jnp.concatenate([stats[2 * c][:, :cols.start], m_new], axis=1),
                        jnp.concatenate([stats[2 * c + 1][:, :cols.start], l_new], axis=1)]
        return tuple(out)

    full = slice(0, TQ)
    late = slice(TQ - TK, TQ)
    neg = jnp.full((1, TQ), -jnp.inf, F32)
    zero = jnp.zeros((1, TQ), F32)

    def body(j, carry):
        stats, mb_a = carry[:4], carry[4:]
        mb_b = scores(2 * j + 1, sb_ref, full, False)
        stats = softmax_pv(2 * j, sa_ref, mb_a, stats, full)
        mb_a = scores(2 * j + 2, sa_ref, full, False)
        stats = softmax_pv(2 * j + 1, sb_ref, mb_b, stats, full)
        return stats + mb_a

    @pl.when(jnp.logical_and(pl.program_id(0) == 0, qi == 0))
    def _():
        first = scores(0, sa_ref, full, False)
        mb_ref[0], mb_ref[1] = first

    mb_a = (mb_ref[0], mb_ref[1])
    carry = lax.fori_loop(0, qi, body, (neg, zero, neg, zero) + mb_a)
    stats = carry[:4]
    mb_b = scores(2 * qi + 1, sb_ref, late, True)
    stats = softmax_pv(2 * qi, sa_ref, None, stats, full, diag_cols=TK)
    mb_ref[0], mb_ref[1] = scores(0, sa_ref, full, False, upcoming=True)
    _, l0, _, l1 = softmax_pv(2 * qi + 1, sb_ref, mb_b, stats, late)

    lam = (jnp.exp(jnp.sum(lq1_ref[...] * lk1_ref[...], axis=-1, keepdims=True))
           - jnp.exp(jnp.sum(lq2_ref[...] * lk2_ref[...], axis=-1, keepdims=True))
           + lam_init)
    o_t = acc_ref[0] * (1.0 / l0) - lam * (acc_ref[1] * (1.0 / l1))
    o = o_t.T
    r = lax.rsqrt(jnp.mean(o * o, axis=-1, keepdims=True) + EPS)
    y = o * r * sg_ref[...] * (1.0 - lam_init)
    o_ref[...] = (y * _silu(g_ref[...].astype(F32))).astype(BF16)


def diff_attention(proj, v_t, lq1, lk1, lq2, lk2, subln_g, lam_init):
    S = proj.shape[0]
    TQ, TK = ATT_TQ, ATT_T
    assert TQ == 2 * TK
    qb, kb, gb = OFF_QA // DA_VDIM, OFF_KA // DA_VDIM, OFF_GA // DA_VDIM
    vec = pl.BlockSpec((1, DA_HEAD), lambda h, i: (0, 0))
    n_q = S // TQ

    def upcoming(h, i):
        last = i == n_q - 1
        return jnp.where(last, jnp.minimum(h + 1, DA_HEADS - 1), h), jnp.where(last, 0, i + 1)

    def q_next(h, i):
        hn, qn = upcoming(h, i)
        return qn, qb + hn

    return pl.pallas_call(
        functools.partial(_attn_kernel, lam_init),
        grid=(DA_HEADS, n_q),
        in_specs=[pl.BlockSpec((TQ, DA_VDIM), lambda h, i: (i, qb + h)),
                  pl.BlockSpec((S, DA_VDIM), lambda h, i: (0, kb + h)),
                  pl.BlockSpec((S // TK, DA_VDIM, TK), lambda h, i: (0, h, 0)),
                  pl.BlockSpec((TQ, DA_VDIM), lambda h, i: (i, gb + h)),
                  pl.BlockSpec((TQ, DA_VDIM), q_next),
                  pl.BlockSpec((TK, DA_VDIM), lambda h, i: (0, kb + upcoming(h, i)[0])),
                  vec, vec, vec, vec,
                  pl.BlockSpec((1, DA_VDIM), lambda h, i: (0, 0))],
        out_specs=pl.BlockSpec((TQ, DA_VDIM), lambda h, i: (i, h)),
        out_shape=jax.ShapeDtypeStruct((S, DA_WIDTH), BF16),
        scratch_shapes=[pltpu.VMEM((2, DA_VDIM, TQ), F32),
                        pltpu.VMEM((2, TK, TQ), F32), pltpu.VMEM((2, TK, TQ), F32),
                        pltpu.VMEM((2, 1, TQ), F32)],
        compiler_params=_cparams(("arbitrary", "arbitrary")),
        name="diff_attention",
    )(proj, proj, v_t, proj, proj, proj, lq1.reshape(1, -1), lk1.reshape(1, -1), lq2.reshape(1, -1),
      lk2.reshape(1, -1), subln_g.reshape(1, -1))


def _ret_kernel(q_ref, k_ref, v_ref, g_ref, dm_ref, zeta_ref, xi_ref, cd_ref, o_ref, st_ref):
    @pl.when(pl.program_id(1) == 0)
    def _():
        st_ref[...] = jnp.zeros(st_ref.shape, F32)

    dmat = dm_ref[0]
    zeta = zeta_ref[0]
    xi = xi_ref[0]
    cdecay = cd_ref[0]
    for n in range(RET_TM // CHUNK):
        rows = slice(n * CHUNK, (n + 1) * CHUNK)
        qc = q_ref[rows, :]
        kc = k_ref[rows, :]
        vc = v_ref[rows, :]
        state = st_ref[...]
        inner = lax.dot_general(qc, kc, (((1,), (1,)), ((), ())),
                                preferred_element_type=F32) * dmat
        o = jnp.dot(inner.astype(BF16), vc, preferred_element_type=F32)
        o = o + jnp.dot((qc.astype(F32) * xi).astype(BF16), state.astype(BF16),
                        preferred_element_type=F32)
        kz = (kc.astype(F32) * zeta).astype(BF16)
        st_ref[...] = state * cdecay + lax.dot_general(
            kz, vc, (((0,), (0,)), ((), ())), preferred_element_type=F32)
        r = lax.rsqrt(jnp.mean(o * o, axis=-1, keepdims=True) + EPS)
        o_ref[rows, :] = (o * r * _silu(g_ref[rows, :].astype(F32))).astype(BF16)


def retention_tables():
    lg = jnp.log(1.0 - 2.0 ** (-5.0 - jnp.arange(RET_HEADS, dtype=F32)))
    idx = jnp.arange(CHUNK, dtype=F32)
    rel = idx[:, None] - idx[None, :]
    dmat = jnp.where(rel >= 0, jnp.exp(jnp.maximum(rel, 0.0) * lg[:, None, None]), 0.0)
    zeta = jnp.exp((CHUNK - 1 - idx)[None, :] * lg[:, None])[..., None]
    xi = jnp.exp((idx + 1.0)[None, :] * lg[:, None])[..., None]
    cdecay = jnp.exp(CHUNK * lg).reshape(RET_HEADS, 1, 1)
    return dmat, zeta, xi, cdecay


def retention(proj, tabs):
    S = proj.shape[0]
    qb, kb = OFF_QR // RET_QK_DIM, OFF_KR // RET_QK_DIM
    vb, gb = OFF_VR // RET_V_DIM, OFF_GR // RET_V_DIM
    dmat, zeta, xi, cdecay = tabs
    return pl.pallas_call(
        _ret_kernel,
        grid=(RET_HEADS, S // RET_TM),
        in_specs=[pl.BlockSpec((RET_TM, RET_QK_DIM), lambda h, i: (i, qb + h)),
                  pl.BlockSpec((RET_TM, RET_QK_DIM), lambda h, i: (i, kb + h)),
                  pl.BlockSpec((RET_TM, RET_V_DIM), lambda h, i: (i, vb + h)),
                  pl.BlockSpec((RET_TM, RET_V_DIM), lambda h, i: (i, gb + h)),
                  pl.BlockSpec((1, CHUNK, CHUNK), lambda h, i: (h, 0, 0)),
                  pl.BlockSpec((1, CHUNK, 1), lambda h, i: (h, 0, 0)),
                  pl.BlockSpec((1, CHUNK, 1), lambda h, i: (h, 0, 0)),
                  pl.BlockSpec((1, 1, 1), lambda h, i: (h, 0, 0))],
        out_specs=pl.BlockSpec((RET_TM, RET_V_DIM), lambda h, i: (i, h)),
        out_shape=jax.ShapeDtypeStruct((S, RET_WIDTH), BF16),
        scratch_shapes=[pltpu.VMEM((RET_QK_DIM, RET_V_DIM), F32)],
        compiler_params=_cparams(("arbitrary", "arbitrary")),
        name="retention",
    )(proj, proj, proj, proj, dmat, zeta, xi, cdecay)


def _conv_kernel(a_ref, b_ref, ha_ref, hb_ref, g_ref, w_ref, cb_ref, lg_ref, lb_ref,
                 pw_ref, pb_ref, o_ref, u_ref, z_ref, wb_ref, vb_ref):
    i = pl.program_id(0)
    T, H, R = CONV_T, CONV_HALO, SUBLANES
    for j in range(CONV_K):
        wb_ref[j] = jnp.broadcast_to(w_ref[j:j + 1, :], (R, CONV_CH))
    for n, ref in enumerate((cb_ref, lg_ref, lb_ref)):
        vb_ref[n] = jnp.broadcast_to(ref[...], (R, CONV_CH))
    keep = (i > 0).astype(F32)
    u_ref[0, 0:H, :] = (ha_ref[...].astype(F32) * jax.nn.sigmoid(hb_ref[...].astype(F32))
                        * keep)
    u_ref[0, H:H + T, :] = a_ref[...].astype(F32) * jax.nn.sigmoid(b_ref[...].astype(F32))
    u_all = u_ref[0]
    for b in range(1, SUBLANES):
        u_ref[b] = pltpu.roll(u_all, T + H - b, 0)
    first = H - (CONV_K - 1)
    for r in range(T // R):
        acc = vb_ref[0]
        for j in range(CONV_K):
            a, b = divmod(first + j, SUBLANES)
            lo = r * R + a * SUBLANES
            acc = acc + wb_ref[j] * u_ref[b, lo:lo + R, :]
        mu = jnp.mean(acc, axis=-1, keepdims=True)
        d = acc - mu
        var = jnp.mean(d * d, axis=-1, keepdims=True)
        yn = d * lax.rsqrt(var + LN_EPS) * vb_ref[1] + vb_ref[2]
        z_ref[r * R:(r + 1) * R, :] = _silu(yn)
    out = jnp.dot(z_ref[...].astype(BF16), pw_ref[...], preferred_element_type=F32) + pb_ref[...]
    o_ref[...] = (out * _silu(g_ref[...].astype(F32))).astype(BF16)


def conformer_conv(proj, conv_w, conv_b, ln_g, ln_b, pw_w, pw_b):
    S = proj.shape[0]
    T, H = CONV_T, CONV_HALO
    ab, bb, gb = OFF_CA // CONV_CH, OFF_CB // CONV_CH, OFF_GC // CONV_CH
    hpb = T // H
    vec = pl.BlockSpec((1, CONV_CH), lambda i: (0, 0))
    return pl.pallas_call(
        _conv_kernel,
        grid=(S // T,),
        in_specs=[pl.BlockSpec((T, CONV_CH), lambda i: (i, ab)),
                  pl.BlockSpec((T, CONV_CH), lambda i: (i, bb)),
                  pl.BlockSpec((H, CONV_CH), lambda i: (jnp.maximum(i * hpb - 1, 0), ab)),
                  pl.BlockSpec((H, CONV_CH), lambda i: (jnp.maximum(i * hpb - 1, 0), bb)),
                  pl.BlockSpec((T, CONV_CH), lambda i: (i, gb)),
                  pl.BlockSpec((CONV_K, CONV_CH), lambda i: (0, 0)),
                  vec, vec, vec,
                  pl.BlockSpec((CONV_CH, CONV_CH), lambda i: (0, 0)),
                  vec],
        out_specs=pl.BlockSpec((T, CONV_CH), lambda i: (i, 0)),
        out_shape=jax.ShapeDtypeStruct((S, CONV_CH), BF16),
        scratch_shapes=[pltpu.VMEM((SUBLANES, T + H, CONV_CH), F32),
                        pltpu.VMEM((T, CONV_CH), F32),
                        pltpu.VMEM((CONV_K, SUBLANES, CONV_CH), F32),
                        pltpu.VMEM((3, SUBLANES, CONV_CH), F32)],
        compiler_params=_cparams(("arbitrary",)),
        name="conformer_conv",
    )(proj, proj, proj, proj, proj, conv_w, conv_b.reshape(1, -1), ln_g.reshape(1, -1),
      ln_b.reshape(1, -1), pw_w, pw_b.reshape(1, -1))


def _outproj_kernel(ya_ref, yr_ref, yc_ref, w_ref, o_ref):
    a_end = DA_WIDTH
    r_end = DA_WIDTH + RET_WIDTH
    acc = jnp.dot(ya_ref[...], w_ref[0:a_end, :].astype(BF16), preferred_element_type=F32)
    acc = acc + jnp.dot(yr_ref[...], w_ref[a_end:r_end, :].astype(BF16),
                        preferred_element_type=F32)
    acc = acc + jnp.dot(yc_ref[...], w_ref[r_end:, :].astype(BF16),
                        preferred_element_type=F32)
    o_ref[...] = acc.astype(BF16)


def out_projection(ya, yr, yc, w_all, layer):
    S = ya.shape[0]
    _, K, N = w_all.shape
    once = pl.Buffered(1)
    return pl.pallas_call(
        _outproj_kernel,
        grid=(S // OUT_TM, N // OUT_TN),
        in_specs=[pl.BlockSpec((OUT_TM, DA_WIDTH), lambda i, j: (i, 0), pipeline_mode=once),
                  pl.BlockSpec((OUT_TM, RET_WIDTH), lambda i, j: (i, 0), pipeline_mode=once),
                  pl.BlockSpec((OUT_TM, CONV_CH), lambda i, j: (i, 0), pipeline_mode=once),
                  pl.BlockSpec((None, K, OUT_TN), lambda i, j: (layer, 0, j))],
        out_specs=pl.BlockSpec((OUT_TM, OUT_TN), lambda i, j: (i, j)),
        out_shape=jax.ShapeDtypeStruct((S, N), BF16),
        compiler_params=_cparams(("arbitrary", "arbitrary")),
        name="out_projection",
    )(ya, yr, yc, w_all)


def kernel(x, pre_norm_g, w_in, lam_q1, lam_k1, lam_q2, lam_k2, diff_subln_g, conv_w, conv_b,
           conv_ln_g, conv_ln_b, conv_pw_w, conv_pw_b, w_out, post_norm_g):
    B, S, D = x.shape
    assert (B, S, D) == (1, SEQ, D_MODEL)
    xs = x.reshape(S, D)
    rtabs = rope_tables(S)
    ret_tabs = retention_tables()
    h = prenorm(xs, pre_norm_g[0])
    for i in range(DEPTH):
        lam_init = 0.8 - 0.6 * math.exp(-0.3 * i)
        proj, v_t = in_projection(h, w_in, i, rtabs)
        ya = diff_attention(proj, v_t, lam_q1[i], lam_k1[i], lam_q2[i], lam_k2[i],
                            diff_subln_g[i], lam_init)
        yr = retention(proj, ret_tabs)
        yc = conformer_conv(proj, conv_w[i], conv_b[i], conv_ln_g[i], conv_ln_b[i],
                            conv_pw_w[i].astype(BF16), conv_pw_b[i])
        mix = out_projection(ya, yr, yc, w_out, i)
        g_next = pre_norm_g[(i + 1) % DEPTH]
        xs, h = post_norm_residual(mix, xs, post_norm_g[i], g_next)
    return xs.reshape(B, S, D)
```

```python
import functools
import math

import jax
import jax.numpy as jnp
from jax import lax
from jax.experimental import pallas as pl
from jax.experimental.pallas import tpu as pltpu

D_MODEL = 4096
SEQ = 8192
DEPTH = 4
DA_WIDTH = 2048
DA_HEAD = 128
DA_HEADS = 8
DA_VDIM = 2 * DA_HEAD
ROPE_DIM = 32
ROPE_THETA = 500000.0
RET_WIDTH = 1024
RET_HEADS = 4
RET_V_DIM = 256
RET_QK_DIM = 128
RET_QK_WIDTH = 512
RET_THETA = 10000.0
CONV_CH = 1024
CONV_K = 31
CHUNK = 128
IN_COLS = 14336
EPS = 1e-6
LN_EPS = 1e-5

OFF_QA = 0
OFF_KA = 2048
OFF_VA = 4096
OFF_GA = 6144
OFF_QR = 8192
OFF_KR = 8704
OFF_VR = 9216
OFF_GR = 10240
OFF_CA = 11264
OFF_CB = 12288
OFF_GC = 13312

LANES = 128
SUBLANES = 8

PRE_TM = 256
INP_TM = 1024
INP_TN = 512
ATT_T = 512
ATT_TQ = 1024
RET_TM = 1024
CONV_T = 512
CONV_HALO = 32
OUT_TM = 2048
OUT_TN = 512
VMEM_LIMIT = 56 * 1024 * 1024
LOG2E = math.log2(math.e)

F32 = jnp.float32
BF16 = jnp.bfloat16


def _cparams(sem):
    return pltpu.CompilerParams(dimension_semantics=sem, vmem_limit_bytes=VMEM_LIMIT)


def _silu(x):
    return x * jax.nn.sigmoid(x)


def _prenorm_kernel(x_ref, g_ref, h_ref):
    x = x_ref[...]
    r = lax.rsqrt(jnp.mean(x * x, axis=-1, keepdims=True) + EPS)
    h_ref[...] = (x * r * g_ref[...]).astype(BF16)


def prenorm(x, g):
    S, D = x.shape
    return pl.pallas_call(
        _prenorm_kernel,
        grid=(S // PRE_TM,),
        in_specs=[pl.BlockSpec((PRE_TM, D), lambda i: (i, 0)),
                  pl.BlockSpec((1, D), lambda i: (0, 0))],
        out_specs=pl.BlockSpec((PRE_TM, D), lambda i: (i, 0)),
        out_shape=jax.ShapeDtypeStruct((S, D), BF16),
        compiler_params=_cparams(("arbitrary",)),
        name="prenorm",
    )(x, g.reshape(1, D))


def _post_kernel(mix_ref, x_ref, gp_ref, gn_ref, xo_ref, h_ref):
    mix = mix_ref[...].astype(F32)
    r = lax.rsqrt(jnp.mean(mix * mix, axis=-1, keepdims=True) + EPS)
    xn = x_ref[...] + mix * r * gp_ref[...]
    xo_ref[...] = xn
    r2 = lax.rsqrt(jnp.mean(xn * xn, axis=-1, keepdims=True) + EPS)
    h_ref[...] = (xn * r2 * gn_ref[...]).astype(BF16)


def post_norm_residual(mix, x, g_post, g_next):
    S, D = x.shape
    row = pl.BlockSpec((PRE_TM, D), lambda i: (i, 0))
    vec = pl.BlockSpec((1, D), lambda i: (0, 0))
    return pl.pallas_call(
        _post_kernel,
        grid=(S // PRE_TM,),
        in_specs=[row, row, vec, vec],
        out_specs=[row, row],
        out_shape=[jax.ShapeDtypeStruct((S, D), F32), jax.ShapeDtypeStruct((S, D), BF16)],
        compiler_params=_cparams(("arbitrary",)),
        name="post_norm_residual",
    )(mix, x, g_post.reshape(1, D), g_next.reshape(1, D))


ROT_DIFF, ROT_RET = range(2)


def _inproj_kernel(h_ref, w_ref, tab_ref, o_ref, vt_ref, acc0_ref, acc1_ref):
    t = pl.program_id(0)
    n_j = IN_COLS // INP_TN
    col0 = (jnp.maximum(t - 1, 0) % n_j) * INP_TN
    is_diff = col0 < OFF_VA
    is_ret = jnp.logical_and(col0 >= OFF_QR, col0 < OFF_VR)
    half = ROPE_DIM // 2
    shift1 = jnp.where(is_diff, LANES - half, jnp.where(is_ret, RET_QK_DIM // 2, 0))
    shift2 = jnp.where(is_diff, half, 0)
    is_kr = jnp.logical_and(col0 >= OFF_KR, col0 < OFF_VR)
    scale = jnp.where(col0 < OFF_KA, DA_HEAD ** -0.5 * LOG2E,
                      jnp.where(is_kr, RET_QK_DIM ** -0.5, 1.0)).astype(F32)

    @pl.when(t == 0)
    def _():
        acc1_ref[...] = jnp.zeros(acc1_ref.shape, F32)

    def step(acc_w, acc_r, rotary):
        acc_w[...] = jnp.dot(h_ref[...], w_ref[...].astype(BF16), preferred_element_type=F32)
        if rotary:
            c, s1, s2 = tab_ref[0], tab_ref[1], tab_ref[2]
            for g in range(INP_TN // LANES):
                xg = acc_r[:, g * LANES:(g + 1) * LANES]
                y = xg * c + pltpu.roll(xg, shift1, 1) * s1 + pltpu.roll(xg, shift2, 1) * s2
                o_ref[:, g * LANES:(g + 1) * LANES] = (y * scale).astype(BF16)
        else:
            o_ref[...] = acc_r[...].astype(BF16)

            @pl.when(jnp.logical_and(col0 >= OFF_VA, col0 < OFF_GA))
            def _():
                acc_t = acc_r[...].T
                for b in range(INP_TM // ATT_T):
                    vt_ref[b] = acc_t[:, b * ATT_T:(b + 1) * ATT_T].astype(BF16)

    even = t % 2 == 0
    rot = jnp.logical_or(is_diff, is_ret)
    for parity, (acc_w, acc_r) in ((even, (acc0_ref, acc1_ref)),
                                   (jnp.logical_not(even), (acc1_ref, acc0_ref))):
        for cond, rotary in ((rot, True), (jnp.logical_not(rot), False)):
            pl.when(jnp.logical_and(parity, cond))(
                functools.partial(step, acc_w, acc_r, rotary))


def in_projection(h, w_all, layer, tabs):
    S, D = h.shape
    N = w_all.shape[2]
    for off in (OFF_KA, OFF_VA, OFF_GA, OFF_QR, OFF_KR, OFF_VR):
        assert off % INP_TN == 0
    n_i, n_j = S // INP_TM, N // INP_TN
    n_tiles = n_i * n_j
    va_first = OFF_VA // INP_TN
    va_last = OFF_GA // INP_TN - 1

    def cur(t):
        tc = jnp.minimum(t, n_tiles - 1)
        return tc // n_j, tc % n_j

    def prev(t):
        tp = jnp.maximum(t - 1, 0)
        return tp // n_j, tp % n_j

    def vt_index(t):
        i, j = prev(t)
        return i, jnp.clip(j, va_first, va_last) - va_first, 0

    def tab_index(t):
        i, j = prev(t)
        col0 = j * INP_TN
        kind = jnp.where(jnp.logical_and(col0 >= OFF_GA, col0 < OFF_VR), ROT_RET, ROT_DIFF)
        row = jnp.where(col0 >= OFF_VR, jnp.minimum(i + 1, n_i - 1), i)
        return kind, 0, row, 0

    return pl.pallas_call(
        _inproj_kernel,
        grid=(n_tiles + 1,),
        in_specs=[pl.BlockSpec((INP_TM, D), lambda t: (cur(t)[0], 0)),
                  pl.BlockSpec((None, D, INP_TN), lambda t: (layer, 0, cur(t)[1])),
                  pl.BlockSpec((None, 3, INP_TM, LANES), tab_index)],
        out_specs=[pl.BlockSpec((INP_TM, INP_TN), prev),
                   pl.BlockSpec((INP_TM // ATT_T, INP_TN, ATT_T), vt_index)],
        out_shape=[jax.ShapeDtypeStruct((S, N), BF16),
                   jax.ShapeDtypeStruct((S // ATT_T, DA_WIDTH, ATT_T), BF16)],
        scratch_shapes=[pltpu.VMEM((INP_TM, INP_TN), F32), pltpu.VMEM((INP_TM, INP_TN), F32)],
        compiler_params=_cparams(("arbitrary",)),
        name="in_projection",
    )(h, w_all, tabs)


def rope_tables(S):
    pos = jnp.arange(S, dtype=F32)
    half = ROPE_DIM // 2
    inv = 1.0 / (ROPE_THETA ** (jnp.arange(0, ROPE_DIM, 2, dtype=F32) / ROPE_DIM))
    ang = pos[:, None] * inv[None, :]
    cos, sin = jnp.cos(ang), jnp.sin(ang)
    one = jnp.ones((S, LANES), F32)
    zero = jnp.zeros((S, LANES), F32)
    ca = jnp.concatenate([cos, cos, one[:, ROPE_DIM:]], axis=1)
    sa = jnp.concatenate([-sin, zero[:, half:]], axis=1)
    sb = jnp.concatenate([zero[:, :half], sin, zero[:, ROPE_DIM:]], axis=1)
    inv_r = 1.0 / (RET_THETA ** (jnp.arange(0, RET_QK_DIM, 2, dtype=F32) / RET_QK_DIM))
    ang_r = pos[:, None] * inv_r[None, :]
    cos_r, sin_r = jnp.cos(ang_r), jnp.sin(ang_r)
    cr = jnp.concatenate([cos_r, cos_r], axis=1)
    sr = jnp.concatenate([-sin_r, sin_r], axis=1)
    kinds = {ROT_DIFF: (ca, sa, sb), ROT_RET: (cr, sr, zero)}
    return jnp.stack([jnp.stack(kinds[k]) for k in range(len(kinds))])


def _attn_kernel(lam_init, q_ref, k_ref, vt_ref, g_ref, qn_ref, k0n_ref, lq1_ref, lk1_ref,
                 lq2_ref, lk2_ref, sg_ref, o_ref, acc_ref, sa_ref, sb_ref, mb_ref):
    TQ, TK = ATT_TQ, ATT_T
    NT = (((1,), (1,)), ((), ()))
    qi = pl.program_id(1)
    acc_ref[...] = jnp.zeros(acc_ref.shape, F32)

    def scores(kb, s_ref, cols, diag, upcoming=False):
        start = pl.multiple_of(kb * TK, TK)
        n = cols.stop - cols.start
        mb = []
        for c in range(2):
            comp = slice(c * DA_HEAD, (c + 1) * DA_HEAD)
            k = k0n_ref[:, comp] if upcoming else k_ref[pl.ds(start, TK), comp]
            q = qn_ref[cols, comp] if upcoming else q_ref[cols, comp]
            s = lax.dot_general(k, q, NT, preferred_element_type=F32)
            if diag:
                key = lax.broadcasted_iota(jnp.int32, (TK, n), 0)
                qry = lax.broadcasted_iota(jnp.int32, (TK, n), 1)
                s = jnp.where(key <= qry, s, -jnp.inf)
            s_ref[c, :, cols] = s
            mb.append(jnp.max(s, axis=0, keepdims=True))
        return tuple(mb)

    def softmax_pv(kb, s_ref, mb, stats, cols, diag_cols=0):
        vt = vt_ref[kb]
        n = cols.stop - cols.start
        out = []
        for c in range(2):
            m_prev, l_prev = stats[2 * c][:, cols], stats[2 * c + 1][:, cols]
            s = s_ref[c, :, cols]
            m_blk = mb[c] if mb is not None else None
            if diag_cols:
                key = lax.broadcasted_iota(jnp.int32, (TK, n), 0)
                qry = lax.broadcasted_iota(jnp.int32, (TK, n), 1)
                s = jnp.where(jnp.logical_or(key <= qry, qry >= diag_cols), s, -jnp.inf)
                m_blk = jnp.max(s, axis=0, keepdims=True)
            m_new = jnp.maximum(m_prev, m_blk)
            alpha = jnp.exp2(m_prev - m_new)
            p = jnp.exp2(s - m_new)
            l_new = alpha * l_prev + jnp.sum(p, axis=0, keepdims=True)
            acc_ref[c, :, cols] = alpha * acc_ref[c, :, cols] + jnp.dot(
                vt, p.astype(BF16), preferred_element_type=F32)
            if n == TQ:
                out += [m_new, l_new]
            else:
                out += [jnp.concatenate([stats[2 * c][:, :cols.start], m_new], axis=1),
                        jnp.concatenate([stats[2 * c + 1][:, :cols.start], l_new], axis=1)]
        return tuple(out)

    full = slice(0, TQ)
    late = slice(TQ - TK, TQ)
    neg = jnp.full((1, TQ), -jnp.inf, F32)
    zero = jnp.zeros((1, TQ), F32)

    def body(j, carry):
        stats, mb_a = carry[:4], carry[4:]
        mb_b = scores(2 * j + 1, sb_ref, full, False)
        stats = softmax_pv(2 * j, sa_ref, mb_a, stats, full)
        mb_a = scores(2 * j + 2, sa_ref, full, False)
        stats = softmax_pv(2 * j + 1, sb_ref, mb_b, stats, full)
        return stats + mb_a

    @pl.when(jnp.logical_and(pl.program_id(0) == 0, qi == 0))
    def _():
        first = scores(0, sa_ref, full, False)
        mb_ref[0], mb_ref[1] = first

    mb_a = (mb_ref[0], mb_ref[1])
    carry = lax.fori_loop(0, qi, body, (neg, zero, neg, zero) + mb_a)
    stats = carry[:4]
    mb_b = scores(2 * qi + 1, sb_ref, late, True)
    stats = softmax_pv(2 * qi, sa_ref, None, stats, full, diag_cols=TK)
    mb_ref[0], mb_ref[1] = scores(0, sa_ref, full, False, upcoming=True)
    _, l0, _, l1 = softmax_pv(2 * qi + 1, sb_ref, mb_b, stats, late)

    lam = (jnp.exp(jnp.sum(lq1_ref[...] * lk1_ref[...], axis=-1, keepdims=True))
           - jnp.exp(jnp.sum(lq2_ref[...] * lk2_ref[...], axis=-1, keepdims=True))
           + lam_init)
    o_t = acc_ref[0] * (1.0 / l0) - lam * (acc_ref[1] * (1.0 / l1))
    o = o_t.T
    r = lax.rsqrt(jnp.mean(o * o, axis=-1, keepdims=True) + EPS)
    y = o * r * sg_ref[...] * (1.0 - lam_init)
    o_ref[...] = (y * _silu(g_ref[...].astype(F32))).astype(BF16)


def diff_attention(proj, v_t, lq1, lk1, lq2, lk2, subln_g, lam_init):
    S = proj.shape[0]
    TQ, TK = ATT_TQ, ATT_T
    assert TQ == 2 * TK
    qb, kb, gb = OFF_QA // DA_VDIM, OFF_KA // DA_VDIM, OFF_GA // DA_VDIM
    vec = pl.BlockSpec((1, DA_HEAD), lambda h, i: (0, 0))
    n_q = S // TQ

    def upcoming(h, i):
        last = i == n_q - 1
        return jnp.where(last, jnp.minimum(h + 1, DA_HEADS - 1), h), jnp.where(last, 0, i + 1)

    def q_next(h, i):
        hn, qn = upcoming(h, i)
        return qn, qb + hn

    return pl.pallas_call(
        functools.partial(_attn_kernel, lam_init),
        grid=(DA_HEADS, n_q),
        in_specs=[pl.BlockSpec((TQ, DA_VDIM), lambda h, i: (i, qb + h)),
                  pl.BlockSpec((S, DA_VDIM), lambda h, i: (0, kb + h)),
                  pl.BlockSpec((S // TK, DA_VDIM, TK), lambda h, i: (0, h, 0)),
                  pl.BlockSpec((TQ, DA_VDIM), lambda h, i: (i, gb + h)),
                  pl.BlockSpec((TQ, DA_VDIM), q_next),
                  pl.BlockSpec((TK, DA_VDIM), lambda h, i: (0, kb + upcoming(h, i)[0])),
                  vec, vec, vec, vec,
                  pl.BlockSpec((1, DA_VDIM), lambda h, i: (0, 0))],
        out_specs=pl.BlockSpec((TQ, DA_VDIM), lambda h, i: (i, h)),
        out_shape=jax.ShapeDtypeStruct((S, DA_WIDTH), BF16),
        scratch_shapes=[pltpu.VMEM((2, DA_VDIM, TQ), F32),
                        pltpu.VMEM((2, TK, TQ), F32), pltpu.VMEM((2, TK, TQ), F32),
                        pltpu.VMEM((2, 1, TQ), F32)],
        compiler_params=_cparams(("arbitrary", "arbitrary")),
        name="diff_attention",
    )(proj, proj, v_t, proj, proj, proj, lq1.reshape(1, -1), lk1.reshape(1, -1), lq2.reshape(1, -1),
      lk2.reshape(1, -1), subln_g.reshape(1, -1))


def _ret_kernel(q_ref, k_ref, v_ref, g_ref, dm_ref, zeta_ref, xi_ref, cd_ref, o_ref, st_ref):
    @pl.when(pl.program_id(1) == 0)
    def _():
        st_ref[...] = jnp.zeros(st_ref.shape, F32)

    dmat = dm_ref[0]
    zeta = zeta_ref[0]
    xi = xi_ref[0]
    cdecay = cd_ref[0]
    for n in range(RET_TM // CHUNK):
        rows = slice(n * CHUNK, (n + 1) * CHUNK)
        qc = q_ref[rows, :]
        kc = k_ref[rows, :]
        vc = v_ref[rows, :]
        state = st_ref[...]
        inner = lax.dot_general(qc, kc, (((1,), (1,)), ((), ())),
                                preferred_element_type=F32) * dmat
        o = jnp.dot(inner.astype(BF16), vc, preferred_element_type=F32)
        o = o + jnp.dot((qc.astype(F32) * xi).astype(BF16), state.astype(BF16),
                        preferred_element_type=F32)
        kz = (kc.astype(F32) * zeta).astype(BF16)
        st_ref[...] = state * cdecay + lax.dot_general(
            kz, vc, (((0,), (0,)), ((), ())), preferred_element_type=F32)
        r = lax.rsqrt(jnp.mean(o * o, axis=-1, keepdims=True) + EPS)
        o_ref[rows, :] = (o * r * _silu(g_ref[rows, :].astype(F32))).astype(BF16)


def retention_tables():
    lg = jnp.log(1.0 - 2.0 ** (-5.0 - jnp.arange(RET_HEADS, dtype=F32)))
    idx = jnp.arange(CHUNK, dtype=F32)
    rel = idx[:, None] - idx[None, :]
    dmat = jnp.where(rel >= 0, jnp.exp(jnp.maximum(rel, 0.0) * lg[:, None, None]), 0.0)
    zeta = jnp.exp((CHUNK - 1 - idx)[None, :] * lg[:, None])[..., None]
    xi = jnp.exp((idx + 1.0)[None, :] * lg[:, None])[..., None]
    cdecay = jnp.exp(CHUNK * lg).reshape(RET_HEADS, 1, 1)
    return dmat, zeta, xi, cdecay


def retention(proj, tabs):
    S = proj.shape[0]
    qb, kb = OFF_QR // RET_QK_DIM, OFF_KR // RET_QK_DIM
    vb, gb = OFF_VR // RET_V_DIM, OFF_GR // RET_V_DIM
    dmat, zeta, xi, cdecay = tabs
    return pl.pallas_call(
        _ret_kernel,
        grid=(RET_HEADS, S // RET_TM),
        in_specs=[pl.BlockSpec((RET_TM, RET_QK_DIM), lambda h, i: (i, qb + h)),
                  pl.BlockSpec((RET_TM, RET_QK_DIM), lambda h, i: (i, kb + h)),
                  pl.BlockSpec((RET_TM, RET_V_DIM), lambda h, i: (i, vb + h)),
                  pl.BlockSpec((RET_TM, RET_V_DIM), lambda h, i: (i, gb + h)),
                  pl.BlockSpec((1, CHUNK, CHUNK), lambda h, i: (h, 0, 0)),
                  pl.BlockSpec((1, CHUNK, 1), lambda h, i: (h, 0, 0)),
                  pl.BlockSpec((1, CHUNK, 1), lambda h, i: (h, 0, 0)),
                  pl.BlockSpec((1, 1, 1), lambda h, i: (h, 0, 0))],
        out_specs=pl.BlockSpec((RET_TM, RET_V_DIM), lambda h, i: (i, h)),
        out_shape=jax.ShapeDtypeStruct((S, RET_WIDTH), BF16),
        scratch_shapes=[pltpu.VMEM((RET_QK_DIM, RET_V_DIM), F32)],
        compiler_params=_cparams(("arbitrary", "arbitrary")),
        name="retention",
    )(proj, proj, proj, proj, dmat, zeta, xi, cdecay)


def _conv_kernel(a_ref, b_ref, ha_ref, hb_ref, g_ref, w_ref, cb_ref, lg_ref, lb_ref,
                 pw_ref, pb_ref, o_ref, u_ref, z_ref, wb_ref, vb_ref):
    i = pl.program_id(0)
    T, H, R = CONV_T, CONV_HALO, SUBLANES
    for j in range(CONV_K):
        wb_ref[j] = jnp.broadcast_to(w_ref[j:j + 1, :], (R, CONV_CH))
    for n, ref in enumerate((cb_ref, lg_ref, lb_ref)):
        vb_ref[n] = jnp.broadcast_to(ref[...], (R, CONV_CH))
    keep = (i > 0).astype(F32)
    u_ref[0, 0:H, :] = (ha_ref[...].astype(F32) * jax.nn.sigmoid(hb_ref[...].astype(F32))
                        * keep)
    u_ref[0, H:H + T, :] = a_ref[...].astype(F32) * jax.nn.sigmoid(b_ref[...].astype(F32))
    u_all = u_ref[0]
    for b in range(1, SUBLANES):
        u_ref[b] = pltpu.roll(u_all, T + H - b, 0)
    first = H - (CONV_K - 1)
    for r in range(T // R):
        acc = vb_ref[0]
        for j in range(CONV_K):
            a, b = divmod(first + j, SUBLANES)
            lo = r * R + a * SUBLANES
            acc = acc + wb_ref[j] * u_ref[b, lo:lo + R, :]
        mu = jnp.mean(acc, axis=-1, keepdims=True)
        d = acc - mu
        var = jnp.mean(d * d, axis=-1, keepdims=True)
        yn = d * lax.rsqrt(var + LN_EPS) * vb_ref[1] + vb_ref[2]
        z_ref[r * R:(r + 1) * R, :] = _silu(yn)
    out = jnp.dot(z_ref[...].astype(BF16), pw_ref[...], preferred_element_type=F32) + pb_ref[...]
    o_ref[...] = (out * _silu(g_ref[...].astype(F32))).astype(BF16)


def conformer_conv(proj, conv_w, conv_b, ln_g, ln_b, pw_w, pw_b):
    S = proj.shape[0]
    T, H = CONV_T, CONV_HALO
    ab, bb, gb = OFF_CA // CONV_CH, OFF_CB // CONV_CH, OFF_GC // CONV_CH
    hpb = T // H
    vec = pl.BlockSpec((1, CONV_CH), lambda i: (0, 0))
    return pl.pallas_call(
        _conv_kernel,
        grid=(S // T,),
        in_specs=[pl.BlockSpec((T, CONV_CH), lambda i: (i, ab)),
                  pl.BlockSpec((T, CONV_CH), lambda i: (i, bb)),
                  pl.BlockSpec((H, CONV_CH), lambda i: (jnp.maximum(i * hpb - 1, 0), ab)),
                  pl.BlockSpec((H, CONV_CH), lambda i: (jnp.maximum(i * hpb - 1, 0), bb)),
                  pl.BlockSpec((T, CONV_CH), lambda i: (i, gb)),
                  pl.BlockSpec((CONV_K, CONV_CH), lambda i: (0, 0)),
                  vec, vec, vec,
                  pl.BlockSpec((CONV_CH, CONV_CH), lambda i: (0, 0)),
                  vec],
        out_specs=pl.BlockSpec((T, CONV_CH), lambda i: (i, 0)),
        out_shape=jax.ShapeDtypeStruct((S, CONV_CH), BF16),
        scratch_shapes=[pltpu.VMEM((SUBLANES, T + H, CONV_CH), F32),
                        pltpu.VMEM((T, CONV_CH), F32),
                        pltpu.VMEM((CONV_K, SUBLANES, CONV_CH), F32),
                        pltpu.VMEM((3, SUBLANES, CONV_CH), F32)],
        compiler_params=_cparams(("arbitrary",)),
        name="conformer_conv",
    )(proj, proj, proj, proj, proj, conv_w, conv_b.reshape(1, -1), ln_g.reshape(1, -1),
      ln_b.reshape(1, -1), pw_w, pw_b.reshape(1, -1))


def _outproj_kernel(ya_ref, yr_ref, yc_ref, w_ref, o_ref):
    a_end = DA_WIDTH
    r_end = DA_WIDTH + RET_WIDTH
    acc = jnp.dot(ya_ref[...], w_ref[0:a_end, :].astype(BF16), preferred_element_type=F32)
    acc = acc + jnp.dot(yr_ref[...], w_ref[a_end:r_end, :].astype(BF16),
                        preferred_element_type=F32)
    acc = acc + jnp.dot(yc_ref[...], w_ref[r_end:, :].astype(BF16),
                        preferred_element_type=F32)
    o_ref[...] = acc.astype(BF16)


def out_projection(ya, yr, yc, w_all, layer):
    S = ya.shape[0]
    _, K, N = w_all.shape
    once = pl.Buffered(1)
    return pl.pallas_call(
        _outproj_kernel,
        grid=(S // OUT_TM, N // OUT_TN),
        in_specs=[pl.BlockSpec((OUT_TM, DA_WIDTH), lambda i, j: (i, 0), pipeline_mode=once),
                  pl.BlockSpec((OUT_TM, RET_WIDTH), lambda i, j: (i, 0), pipeline_mode=once),
                  pl.BlockSpec((OUT_TM, CONV_CH), lambda i, j: (i, 0), pipeline_mode=once),
                  pl.BlockSpec((None, K, OUT_TN), lambda i, j: (layer, 0, j))],
        out_specs=pl.BlockSpec((OUT_TM, OUT_TN), lambda i, j: (i, j)),
        out_shape=jax.ShapeDtypeStruct((S, N), BF16),
        compiler_params=_cparams(("arbitrary", "arbitrary")),
        name="out_projection",
    )(ya, yr, yc, w_all)


def kernel(x, pre_norm_g, w_in, lam_q1, lam_k1, lam_q2, lam_k2, diff_subln_g, conv_w, conv_b,
           conv_ln_g, conv_ln_b, conv_pw_w, conv_pw_b, w_out, post_norm_g):
    B, S, D = x.shape
    assert (B, S, D) == (1, SEQ, D_MODEL)
    xs = x.reshape(S, D)
    rtabs = rope_tables(S)
    ret_tabs = retention_tables()
    h = prenorm(xs, pre_norm_g[0])
    for i in range(DEPTH):
        lam_init = 0.8 - 0.6 * math.exp(-0.3 * i)
        proj, v_t = in_projection(h, w_in, i, rtabs)
        ya = diff_attention(proj, v_t, lam_q1[i], lam_k1[i], lam_q2[i], lam_k2[i],
                            diff_subln_g[i], lam_init)
        yr = retention(proj, ret_tabs)
        yc = conformer_conv(proj, conv_w[i], conv_b[i], conv_ln_g[i], conv_ln_b[i],
                            conv_pw_w[i].astype(BF16), conv_pw_b[i])
        mix = out_projection(ya, yr, yc, w_out, i)
        g_next = pre_norm_g[(i + 1) % DEPTH]
        xs, h = post_norm_residual(mix, xs, post_norm_g[i], g_next)
    return xs.reshape(B, S, D)
```

```python
import functools
import math

import jax
import jax.numpy as jnp
from jax import lax
from jax.experimental import pallas as pl
from jax.experimental.pallas import tpu as pltpu

D_MODEL = 4096
SEQ = 8192
DEPTH = 4
DA_WIDTH = 2048
DA_HEAD = 128
DA_HEADS = 8
DA_VDIM = 2 * DA_HEAD
ROPE_DIM = 32
ROPE_THETA = 500000.0
RET_WIDTH = 1024
RET_HEADS = 4
RET_V_DIM = 256
RET_QK_DIM = 128
RET_QK_WIDTH = 512
RET_THETA = 10000.0
CONV_CH = 1024
CONV_K = 31
CHUNK = 128
IN_COLS = 14336
EPS = 1e-6
LN_EPS = 1e-5

OFF_QA = 0
OFF_KA = 2048
OFF_VA = 4096
OFF_GA = 6144
OFF_QR = 8192
OFF_KR = 8704
OFF_VR = 9216
OFF_GR = 10240
OFF_CA = 11264
OFF_CB = 12288
OFF_GC = 13312

LANES = 128
SUBLANES = 8

PRE_TM = 256
INP_TM = 1024
INP_TN = 512
ATT_T = 512
ATT_TQ = 1024
RET_TM = 1024
CONV_T = 512
CONV_HALO = 32
OUT_TM = 2048
OUT_TN = 512
VMEM_LIMIT = 56 * 1024 * 1024
LOG2E = math.log2(math.e)

F32 = jnp.float32
BF16 = jnp.bfloat16


def _cparams(sem):
    return pltpu.CompilerParams(dimension_semantics=sem, vmem_limit_bytes=VMEM_LIMIT)


def _silu(x):
    return x * jax.nn.sigmoid(x)


def _prenorm_kernel(x_ref, g_ref, h_ref):
    x = x_ref[...]
    r = lax.rsqrt(jnp.mean(x * x, axis=-1, keepdims=True) + EPS)
    h_ref[...] = (x * r * g_ref[...]).astype(BF16)


def prenorm(x, g):
    S, D = x.shape
    return pl.pallas_call(
        _prenorm_kernel,
        grid=(S // PRE_TM,),
        in_specs=[pl.BlockSpec((PRE_TM, D), lambda i: (i, 0)),
                  pl.BlockSpec((1, D), lambda i: (0, 0))],
        out_specs=pl.BlockSpec((PRE_TM, D), lambda i: (i, 0)),
        out_shape=jax.ShapeDtypeStruct((S, D), BF16),
        compiler_params=_cparams(("arbitrary",)),
        name="prenorm",
    )(x, g.reshape(1, D))


def _post_kernel(mix_ref, x_ref, gp_ref, gn_ref, xo_ref, h_ref):
    mix = mix_ref[...].astype(F32)
    r = lax.rsqrt(jnp.mean(mix * mix, axis=-1, keepdims=True) + EPS)
    xn = x_ref[...] + mix * r * gp_ref[...]
    xo_ref[...] = xn
    r2 = lax.rsqrt(jnp.mean(xn * xn, axis=-1, keepdims=True) + EPS)
    h_ref[...] = (xn * r2 * gn_ref[...]).astype(BF16)


def post_norm_residual(mix, x, g_post, g_next):
    S, D = x.shape
    row = pl.BlockSpec((PRE_TM, D), lambda i: (i, 0))
    vec = pl.BlockSpec((1, D), lambda i: (0, 0))
    return pl.pallas_call(
        _post_kernel,
        grid=(S // PRE_TM,),
        in_specs=[row, row, vec, vec],
        out_specs=[row, row],
        out_shape=[jax.ShapeDtypeStruct((S, D), F32), jax.ShapeDtypeStruct((S, D), BF16)],
        compiler_params=_cparams(("arbitrary",)),
        name="post_norm_residual",
    )(mix, x, g_post.reshape(1, D), g_next.reshape(1, D))


ROT_DIFF, ROT_RET = range(2)


def _inproj_kernel(h_ref, w_ref, tab_ref, o_ref, vt_ref, acc0_ref, acc1_ref):
    t = pl.program_id(0)
    n_j = IN_COLS // INP_TN
    col0 = (jnp.maximum(t - 1, 0) % n_j) * INP_TN
    is_diff = col0 < OFF_VA
    is_ret = jnp.logical_and(col0 >= OFF_QR, col0 < OFF_VR)
    half = ROPE_DIM // 2
    shift1 = jnp.where(is_diff, LANES - half, jnp.where(is_ret, RET_QK_DIM // 2, 0))
    shift2 = jnp.where(is_diff, half, 0)
    is_kr = jnp.logical_and(col0 >= OFF_KR, col0 < OFF_VR)
    scale = jnp.where(col0 < OFF_KA, DA_HEAD ** -0.5 * LOG2E,
                      jnp.where(is_kr, RET_QK_DIM ** -0.5, 1.0)).astype(F32)

    @pl.when(t == 0)
    def _():
        acc1_ref[...] = jnp.zeros(acc1_ref.shape, F32)

    def step(acc_w, acc_r, rotary):
        acc_w[...] = jnp.dot(h_ref[...], w_ref[...].astype(BF16), preferred_element_type=F32)
        if rotary:
            c, s1, s2 = tab_ref[0], tab_ref[1], tab_ref[2]
            for g in range(INP_TN // LANES):
                xg = acc_r[:, g * LANES:(g + 1) * LANES]
                y = xg * c + pltpu.roll(xg, shift1, 1) * s1 + pltpu.roll(xg, shift2, 1) * s2
                o_ref[:, g * LANES:(g + 1) * LANES] = (y * scale).astype(BF16)
        else:
            o_ref[...] = acc_r[...].astype(BF16)

            @pl.when(jnp.logical_and(col0 >= OFF_VA, col0 < OFF_GA))
            def _():
                acc_t = acc_r[...].T
                for b in range(INP_TM // ATT_T):
                    vt_ref[b] = acc_t[:, b * ATT_T:(b + 1) * ATT_T].astype(BF16)

    even = t % 2 == 0
    rot = jnp.logical_or(is_diff, is_ret)
    for parity, (acc_w, acc_r) in ((even, (acc0_ref, acc1_ref)),
                                   (jnp.logical_not(even), (acc1_ref, acc0_ref))):
        for cond, rotary in ((rot, True), (jnp.logical_not(rot), False)):
            pl.when(jnp.logical_and(parity, cond))(
                functools.partial(step, acc_w, acc_r, rotary))


def in_projection(h, w_all, layer, tabs):
    S, D = h.shape
    N = w_all.shape[2]
    for off in (OFF_KA, OFF_VA, OFF_GA, OFF_QR, OFF_KR, OFF_VR):
        assert off % INP_TN == 0
    n_i, n_j = S // INP_TM, N // INP_TN
    n_tiles = n_i * n_j
    va_first = OFF_VA // INP_TN
    va_last = OFF_GA // INP_TN - 1

    def cur(t):
        tc = jnp.minimum(t, n_tiles - 1)
        return tc // n_j, tc % n_j

    def prev(t):
        tp = jnp.maximum(t - 1, 0)
        return tp // n_j, tp % n_j

    def vt_index(t):
        i, j = prev(t)
        return i, jnp.clip(j, va_first, va_last) - va_first, 0

    def tab_index(t):
        i, j = prev(t)
        col0 = j * INP_TN
        kind = jnp.where(jnp.logical_and(col0 >= OFF_GA, col0 < OFF_VR), ROT_RET, ROT_DIFF)
        row = jnp.where(col0 >= OFF_VR, jnp.minimum(i + 1, n_i - 1), i)
        return kind, 0, row, 0

    return pl.pallas_call(
        _inproj_kernel,
        grid=(n_tiles + 1,),
        in_specs=[pl.BlockSpec((INP_TM, D), lambda t: (cur(t)[0], 0)),
                  pl.BlockSpec((None, D, INP_TN), lambda t: (layer, 0, cur(t)[1])),
                  pl.BlockSpec((None, 3, INP_TM, LANES), tab_index)],
        out_specs=[pl.BlockSpec((INP_TM, INP_TN), prev),
                   pl.BlockSpec((INP_TM // ATT_T, INP_TN, ATT_T), vt_index)],
        out_shape=[jax.ShapeDtypeStruct((S, N), BF16),
                   jax.ShapeDtypeStruct((S // ATT_T, DA_WIDTH, ATT_T), BF16)],
        scratch_shapes=[pltpu.VMEM((INP_TM, INP_TN), F32), pltpu.VMEM((INP_TM, INP_TN), F32)],
        compiler_params=_cparams(("arbitrary",)),
        name="in_projection",
    )(h, w_all, tabs)


def rope_tables(S):
    pos = jnp.arange(S, dtype=F32)[:, None]
    lane = jnp.arange(LANES)[None, :]
    half = ROPE_DIM // 2
    inv = 1.0 / (ROPE_THETA ** (jnp.arange(0, ROPE_DIM, 2, dtype=F32) / ROPE_DIM))
    inv_l = jnp.concatenate([inv, inv, jnp.zeros((LANES - ROPE_DIM,), F32)])
    ang = pos * inv_l[None, :]
    ca, sin = jnp.cos(ang), jnp.sin(ang)
    sa = jnp.where(lane < half, -sin, 0.0)
    sb = jnp.where(jnp.logical_and(lane >= half, lane < ROPE_DIM), sin, 0.0)
    inv_r = 1.0 / (RET_THETA ** (jnp.arange(0, RET_QK_DIM, 2, dtype=F32) / RET_QK_DIM))
    ang_r = pos * jnp.concatenate([inv_r, inv_r])[None, :]
    cr, sin_r = jnp.cos(ang_r), jnp.sin(ang_r)
    sr = jnp.where(lane < RET_QK_DIM // 2, -sin_r, sin_r)
    zero = jnp.zeros((S, LANES), F32)
    kinds = {ROT_DIFF: (ca, sa, sb), ROT_RET: (cr, sr, zero)}
    return jnp.stack([jnp.stack(kinds[k]) for k in range(len(kinds))])


def _attn_kernel(lam_init, q_ref, k_ref, vt_ref, g_ref, qn_ref, k0n_ref, lq1_ref, lk1_ref,
                 lq2_ref, lk2_ref, sg_ref, o_ref, acc_ref, sa_ref, sb_ref, mb_ref):
    TQ, TK = ATT_TQ, ATT_T
    NT = (((1,), (1,)), ((), ()))
    qi = pl.program_id(1)
    acc_ref[...] = jnp.zeros(acc_ref.shape, F32)

    def scores(kb, s_ref, cols, diag, upcoming=False):
        start = pl.multiple_of(kb * TK, TK)
        n = cols.stop - cols.start
        mb = []
        for c in range(2):
            comp = slice(c * DA_HEAD, (c + 1) * DA_HEAD)
            k = k0n_ref[:, comp] if upcoming else k_ref[pl.ds(start, TK), comp]
            q = qn_ref[cols, comp] if upcoming else q_ref[cols, comp]
            s = lax.dot_general(k, q, NT, preferred_element_type=F32)
            if diag:
                key = lax.broadcasted_iota(jnp.int32, (TK, n), 0)
                qry = lax.broadcasted_iota(jnp.int32, (TK, n), 1)
                s = jnp.where(key <= qry, s, -jnp.inf)
            s_ref[c, :, cols] = s
            mb.append(jnp.max(s, axis=0, keepdims=True))
        return tuple(mb)

    def softmax_pv(kb, s_ref, mb, stats, cols, diag_cols=0):
        vt = vt_ref[kb]
        n = cols.stop - cols.start
        out = []
        for c in range(2):
            m_prev, l_prev = stats[2 * c][:, cols], stats[2 * c + 1][:, cols]
            s = s_ref[c, :, cols]
            m_blk = mb[c] if mb is not None else None
            if diag_cols:
                key = lax.broadcasted_iota(jnp.int32, (TK, n), 0)
                qry = lax.broadcasted_iota(jnp.int32, (TK, n), 1)
                s = jnp.where(jnp.logical_or(key <= qry, qry >= diag_cols), s, -jnp.inf)
                m_blk = jnp.max(s, axis=0, keepdims=True)
            m_new = jnp.maximum(m_prev, m_blk)
            alpha = jnp.exp2(m_prev - m_new)
            p = jnp.exp2(s - m_new)
            l_new = alpha * l_prev + jnp.sum(p, axis=0, keepdims=True)
            acc_ref[c, :, cols] = alpha * acc_ref[c, :, cols] + jnp.dot(
                vt, p.astype(BF16), preferred_element_type=F32)
            if n == TQ:
                out += [m_new, l_new]
            else:
                out += [jnp.concatenate([stats[2 * c][:, :cols.start], m_new], axis=1),
                        jnp.concatenate([stats[2 * c + 1][:, :cols.start], l_new], axis=1)]
        return tuple(out)

    full = slice(0, TQ)
    late = slice(TQ - TK, TQ)
    neg = jnp.full((1, TQ), -jnp.inf, F32)
    zero = jnp.zeros((1, TQ), F32)

    def body(j, carry):
        stats, mb_a = carry[:4], carry[4:]
        mb_b = scores(2 * j + 1, sb_ref, full, False)
        stats = softmax_pv(2 * j, sa_ref, mb_a, stats, full)
        mb_a = scores(2 * j + 2, sa_ref, full, False)
        stats = softmax_pv(2 * j + 1, sb_ref, mb_b, stats, full)
        return stats + mb_a

    @pl.when(jnp.logical_and(pl.program_id(0) == 0, qi == 0))
    def _():
        first = scores(0, sa_ref, full, False)
        mb_ref[0], mb_ref[1] = first

    mb_a = (mb_ref[0], mb_ref[1])
    carry = lax.fori_loop(0, qi, body, (neg, zero, neg, zero) + mb_a)
    stats = carry[:4]
    mb_b = scores(2 * qi + 1, sb_ref, late, True)
    stats = softmax_pv(2 * qi, sa_ref, None, stats, full, diag_cols=TK)
    mb_ref[0], mb_ref[1] = scores(0, sa_ref, full, False, upcoming=True)
    _, l0, _, l1 = softmax_pv(2 * qi + 1, sb_ref, mb_b, stats, late)

    lam = (jnp.exp(jnp.sum(lq1_ref[...] * lk1_ref[...], axis=-1, keepdims=True))
           - jnp.exp(jnp.sum(lq2_ref[...] * lk2_ref[...], axis=-1, keepdims=True))
           + lam_init)
    o_t = acc_ref[0] * (1.0 / l0) - lam * (acc_ref[1] * (1.0 / l1))
    o = o_t.T
    r = lax.rsqrt(jnp.mean(o * o, axis=-1, keepdims=True) + EPS)
    y = o * r * sg_ref[...] * (1.0 - lam_init)
    o_ref[...] = (y * _silu(g_ref[...].astype(F32))).astype(BF16)


def diff_attention(proj, v_t, lq1, lk1, lq2, lk2, subln_g, lam_init):
    S = proj.shape[0]
    TQ, TK = ATT_TQ, ATT_T
    assert TQ == 2 * TK
    qb, kb, gb = OFF_QA // DA_VDIM, OFF_KA // DA_VDIM, OFF_GA // DA_VDIM
    vec = pl.BlockSpec((1, DA_HEAD), lambda h, i: (0, 0))
    n_q = S // TQ

    def upcoming(h, i):
        last = i == n_q - 1
        return jnp.where(last, jnp.minimum(h + 1, DA_HEADS - 1), h), jnp.where(last, 0, i + 1)

    def q_next(h, i):
        hn, qn = upcoming(h, i)
        return qn, qb + hn

    return pl.pallas_call(
        functools.partial(_attn_kernel, lam_init),
        grid=(DA_HEADS, n_q),
        in_specs=[pl.BlockSpec((TQ, DA_VDIM), lambda h, i: (i, qb + h)),
                  pl.BlockSpec((S, DA_VDIM), lambda h, i: (0, kb + h)),
                  pl.BlockSpec((S // TK, DA_VDIM, TK), lambda h, i: (0, h, 0)),
                  pl.BlockSpec((TQ, DA_VDIM), lambda h, i: (i, gb + h)),
                  pl.BlockSpec((TQ, DA_VDIM), q_next),
                  pl.BlockSpec((TK, DA_VDIM), lambda h, i: (0, kb + upcoming(h, i)[0])),
                  vec, vec, vec, vec,
                  pl.BlockSpec((1, DA_VDIM), lambda h, i: (0, 0))],
        out_specs=pl.BlockSpec((TQ, DA_VDIM), lambda h, i: (i, h)),
        out_shape=jax.ShapeDtypeStruct((S, DA_WIDTH), BF16),
        scratch_shapes=[pltpu.VMEM((2, DA_VDIM, TQ), F32),
                        pltpu.VMEM((2, TK, TQ), F32), pltpu.VMEM((2, TK, TQ), F32),
                        pltpu.VMEM((2, 1, TQ), F32)],
        compiler_params=_cparams(("arbitrary", "arbitrary")),
        name="diff_attention",
    )(proj, proj, v_t, proj, proj, proj, lq1.reshape(1, -1), lk1.reshape(1, -1), lq2.reshape(1, -1),
      lk2.reshape(1, -1), subln_g.reshape(1, -1))


def _ret_kernel(q_ref, k_ref, v_ref, g_ref, dm_ref, zeta_ref, xi_ref, cd_ref, o_ref, st_ref):
    @pl.when(pl.program_id(0) == 0)
    def _():
        st_ref[...] = jnp.zeros(st_ref.shape, F32)

    for n in range(RET_TM // CHUNK):
        rows = slice(n * CHUNK, (n + 1) * CHUNK)
        for h in range(RET_HEADS):
            qk = slice(h * RET_QK_DIM, (h + 1) * RET_QK_DIM)
            vv = slice(h * RET_V_DIM, (h + 1) * RET_V_DIM)
            qc = q_ref[rows, qk]
            kc = k_ref[rows, qk]
            vc = v_ref[rows, vv]
            state = st_ref[h]
            inner = lax.dot_general(qc, kc, (((1,), (1,)), ((), ())),
                                    preferred_element_type=F32) * dm_ref[h]
            o = jnp.dot(inner.astype(BF16), vc, preferred_element_type=F32)
            o = o + jnp.dot((qc.astype(F32) * xi_ref[h]).astype(BF16), state.astype(BF16),
                            preferred_element_type=F32)
            kz = (kc.astype(F32) * zeta_ref[h]).astype(BF16)
            st_ref[h] = state * cd_ref[h] + lax.dot_general(
                kz, vc, (((0,), (0,)), ((), ())), preferred_element_type=F32)
            r = lax.rsqrt(jnp.mean(o * o, axis=-1, keepdims=True) + EPS)
            o_ref[rows, vv] = (o * r * _silu(g_ref[rows, vv].astype(F32))).astype(BF16)


def retention_tables():
    lg = jnp.log(1.0 - 2.0 ** (-5.0 - jnp.arange(RET_HEADS, dtype=F32)))
    idx = jnp.arange(CHUNK, dtype=F32)
    rel = idx[:, None] - idx[None, :]
    dmat = jnp.where(rel >= 0, jnp.exp(jnp.maximum(rel, 0.0) * lg[:, None, None]), 0.0)
    zeta = jnp.exp((CHUNK - 1 - idx)[None, :] * lg[:, None])[..., None]
    xi = jnp.exp((idx + 1.0)[None, :] * lg[:, None])[..., None]
    cdecay = jnp.exp(CHUNK * lg).reshape(RET_HEADS, 1, 1)
    return dmat, zeta, xi, cdecay


def retention(proj, tabs):
    S = proj.shape[0]
    qb, kb = OFF_QR // RET_QK_WIDTH, OFF_KR // RET_QK_WIDTH
    vb, gb = OFF_VR // RET_WIDTH, OFF_GR // RET_WIDTH
    dmat, zeta, xi, cdecay = tabs
    whole = lambda i: (0, 0, 0)
    return pl.pallas_call(
        _ret_kernel,
        grid=(S // RET_TM,),
        in_specs=[pl.BlockSpec((RET_TM, RET_QK_WIDTH), lambda i: (i, qb)),
                  pl.BlockSpec((RET_TM, RET_QK_WIDTH), lambda i: (i, kb)),
                  pl.BlockSpec((RET_TM, RET_WIDTH), lambda i: (i, vb)),
                  pl.BlockSpec((RET_TM, RET_WIDTH), lambda i: (i, gb)),
                  pl.BlockSpec((RET_HEADS, CHUNK, CHUNK), whole),
                  pl.BlockSpec((RET_HEADS, CHUNK, 1), whole),
                  pl.BlockSpec((RET_HEADS, CHUNK, 1), whole),
                  pl.BlockSpec((RET_HEADS, 1, 1), whole)],
        out_specs=pl.BlockSpec((RET_TM, RET_WIDTH), lambda i: (i, 0)),
        out_shape=jax.ShapeDtypeStruct((S, RET_WIDTH), BF16),
        scratch_shapes=[pltpu.VMEM((RET_HEADS, RET_QK_DIM, RET_V_DIM), F32)],
        compiler_params=_cparams(("arbitrary",)),
        name="retention",
    )(proj, proj, proj, proj, dmat, zeta, xi, cdecay)


def _conv_kernel(a_ref, b_ref, ha_ref, hb_ref, g_ref, w_ref, cb_ref, lg_ref, lb_ref,
                 pw_ref, pb_ref, o_ref, u_ref, z_ref, wb_ref, vb_ref):
    i = pl.program_id(0)
    T, H, R = CONV_T, CONV_HALO, SUBLANES
    for j in range(CONV_K):
        wb_ref[j] = jnp.broadcast_to(w_ref[j:j + 1, :], (R, CONV_CH))
    for n, ref in enumerate((cb_ref, lg_ref, lb_ref)):
        vb_ref[n] = jnp.broadcast_to(ref[...], (R, CONV_CH))
    keep = (i > 0).astype(F32)
    u_ref[0, 0:H, :] = (ha_ref[...].astype(F32) * jax.nn.sigmoid(hb_ref[...].astype(F32))
                        * keep)
    u_ref[0, H:H + T, :] = a_ref[...].astype(F32) * jax.nn.sigmoid(b_ref[...].astype(F32))
    u_all = u_ref[0]
    for b in range(1, SUBLANES):
        u_ref[b] = pltpu.roll(u_all, T + H - b, 0)
    first = H - (CONV_K - 1)
    for r in range(T // R):
        acc = vb_ref[0]
        for j in range(CONV_K):
            a, b = divmod(first + j, SUBLANES)
            lo = r * R + a * SUBLANES
            acc = acc + wb_ref[j] * u_ref[b, lo:lo + R, :]
        mu = jnp.mean(acc, axis=-1, keepdims=True)
        d = acc - mu
        var = jnp.mean(d * d, axis=-1, keepdims=True)
        yn = d * lax.rsqrt(var + LN_EPS) * vb_ref[1] + vb_ref[2]
        z_ref[r * R:(r + 1) * R, :] = _silu(yn)
    out = jnp.dot(z_ref[...].astype(BF16), pw_ref[...], preferred_element_type=F32) + pb_ref[...]
    o_ref[...] = (out * _silu(g_ref[...].astype(F32))).astype(BF16)


def conformer_conv(proj, conv_w, conv_b, ln_g, ln_b, pw_w, pw_b):
    S = proj.shape[0]
    T, H = CONV_T, CONV_HALO
    ab, bb, gb = OFF_CA // CONV_CH, OFF_CB // CONV_CH, OFF_GC // CONV_CH
    hpb = T // H
    vec = pl.BlockSpec((1, CONV_CH), lambda i: (0, 0))
    return pl.pallas_call(
        _conv_kernel,
        grid=(S // T,),
        in_specs=[pl.BlockSpec((T, CONV_CH), lambda i: (i, ab)),
                  pl.BlockSpec((T, CONV_CH), lambda i: (i, bb)),
                  pl.BlockSpec((H, CONV_CH), lambda i: (jnp.maximum(i * hpb - 1, 0), ab)),
                  pl.BlockSpec((H, CONV_CH), lambda i: (jnp.maximum(i * hpb - 1, 0), bb)),
                  pl.BlockSpec((T, CONV_CH), lambda i: (i, gb)),
                  pl.BlockSpec((CONV_K, CONV_CH), lambda i: (0, 0)),
                  vec, vec, vec,
                  pl.BlockSpec((CONV_CH, CONV_CH), lambda i: (0, 0)),
                  vec],
        out_specs=pl.BlockSpec((T, CONV_CH), lambda i: (i, 0)),
        out_shape=jax.ShapeDtypeStruct((S, CONV_CH), BF16),
        scratch_shapes=[pltpu.VMEM((SUBLANES, T + H, CONV_CH), F32),
                        pltpu.VMEM((T, CONV_CH), F32),
                        pltpu.VMEM((CONV_K, SUBLANES, CONV_CH), F32),
                        pltpu.VMEM((3, SUBLANES, CONV_CH), F32)],
        compiler_params=_cparams(("arbitrary",)),
        name="conformer_conv",
    )(proj, proj, proj, proj, proj, conv_w, conv_b.reshape(1, -1), ln_g.reshape(1, -1),
      ln_b.reshape(1, -1), pw_w, pw_b.reshape(1, -1))


def _outproj_kernel(ya_ref, yr_ref, yc_ref, w_ref, o_ref):
    a_end = DA_WIDTH
    r_end = DA_WIDTH + RET_WIDTH
    acc = jnp.dot(ya_ref[...], w_ref[0:a_end, :].astype(BF16), preferred_element_type=F32)
    acc = acc + jnp.dot(yr_ref[...], w_ref[a_end:r_end, :].astype(BF16),
                        preferred_element_type=F32)
    acc = acc + jnp.dot(yc_ref[...], w_ref[r_end:, :].astype(BF16),
                        preferred_element_type=F32)
    o_ref[...] = acc.astype(BF16)


def out_projection(ya, yr, yc, w_all, layer):
    S = ya.shape[0]
    _, K, N = w_all.shape
    once = pl.Buffered(1)
    return pl.pallas_call(
        _outproj_kernel,
        grid=(S // OUT_TM, N // OUT_TN),
        in_specs=[pl.BlockSpec((OUT_TM, DA_WIDTH), lambda i, j: (i, 0), pipeline_mode=once),
                  pl.BlockSpec((OUT_TM, RET_WIDTH), lambda i, j: (i, 0), pipeline_mode=once),
                  pl.BlockSpec((OUT_TM, CONV_CH), lambda i, j: (i, 0), pipeline_mode=once),
                  pl.BlockSpec((None, K, OUT_TN), lambda i, j: (layer, 0, j))],
        out_specs=pl.BlockSpec((OUT_TM, OUT_TN), lambda i, j: (i, j)),
        out_shape=jax.ShapeDtypeStruct((S, N), BF16),
        compiler_params=_cparams(("arbitrary", "arbitrary")),
        name="out_projection",
    )(ya, yr, yc, w_all)


def kernel(x, pre_norm_g, w_in, lam_q1, lam_k1, lam_q2, lam_k2, diff_subln_g, conv_w, conv_b,
           conv_ln_g, conv_ln_b, conv_pw_w, conv_pw_b, w_out, post_norm_g):
    B, S, D = x.shape
    assert (B, S, D) == (1, SEQ, D_MODEL)
    xs = x.reshape(S, D)
    rtabs = rope_tables(S)
    ret_tabs = retention_tables()
    h = prenorm(xs, pre_norm_g[0])
    for i in range(DEPTH):
        lam_init = 0.8 - 0.6 * math.exp(-0.3 * i)
        proj, v_t = in_projection(h, w_in, i, rtabs)
        ya = diff_attention(proj, v_t, lam_q1[i], lam_k1[i], lam_q2[i], lam_k2[i],
                            diff_subln_g[i], lam_init)
        yr = retention(proj, ret_tabs)
        yc = conformer_conv(proj, conv_w[i], conv_b[i], conv_ln_g[i], conv_ln_b[i],
                            conv_pw_w[i].astype(BF16), conv_pw_b[i])
        mix = out_projection(ya, yr, yc, w_out, i)
        g_next = pre_norm_g[(i + 1) % DEPTH]
        xs, h = post_norm_residual(mix, xs, post_norm_g[i], g_next)
    return xs.reshape(B, S, D)
```
